```python
import jax, jax.numpy as jnp
from jax import lax
import numpy as np

D_MODEL = 1024
BATCH = 8
SEQ = 2048
DEPTH = 1
DEC_BATCH = 128
DEC_SEQ = 4
PAST_LEN = 16384
PAGE_SIZE = 128

GLA_HEADS = 4
GLA_KEY_DIM = D_MODEL // 2
GLA_VAL_DIM = D_MODEL
GLA_DK = GLA_KEY_DIM // GLA_HEADS
GLA_DV = GLA_VAL_DIM // GLA_HEADS
GLA_RANK = 16
GLA_TAU = 16.0
GLA_CHUNK = 64
CONV_DIM = D_MODEL
CONV_WIDTH = 31
PLE_DIM = 256
EPS = 1e-6

SPLITS = (GLA_KEY_DIM, GLA_KEY_DIM, GLA_VAL_DIM, GLA_VAL_DIM, GLA_RANK, 2 * CONV_DIM, CONV_DIM, D_MODEL, D_MODEL)
D_IN = sum(SPLITS)
SPLIT_POINTS = [int(s) for s in np.cumsum(SPLITS)[:-1]]

kernel_name = 'gla_conformer_parallel_hybrid_step'


def rmsnorm(x, g):
    xf = x.astype(jnp.float32)
    y = xf * lax.rsqrt(jnp.mean(xf * xf, axis=-1, keepdims=True) + EPS)
    return (y * g.astype(jnp.float32)).astype(x.dtype)


def layernorm(x, g, b):
    xf = x.astype(jnp.float32)
    mu = jnp.mean(xf, axis=-1, keepdims=True)
    xc = xf - mu
    y = xc * lax.rsqrt(jnp.mean(xc * xc, axis=-1, keepdims=True) + EPS)
    return (y * g.astype(jnp.float32) + b.astype(jnp.float32)).astype(x.dtype)


def gla_chunked(q, k, v, log_a, s0):
    bsz, t_len = q.shape[0], q.shape[1]
    c = min(GLA_CHUNK, t_len)
    n_chunks = -(-t_len // c)
    pad = n_chunks * c - t_len

    def to_chunks(a):
        a = jnp.pad(a.astype(jnp.float32), ((0, 0), (0, pad), (0, 0), (0, 0)))
        return jnp.moveaxis(a.reshape(bsz, n_chunks, c, a.shape[2], a.shape[3]), 1, 0)

    causal = jnp.tril(jnp.ones((c, c), dtype=bool))[None, :, :, None, None]

    def step(s, inp):
        qc, kc, vc, ac = inp
        b = jnp.cumsum(ac, axis=1)
        o_inter = jnp.einsum('bthk,bhkv->bthv', qc * jnp.exp(b), s)
        diff = b[:, :, None] - b[:, None, :]
        decay = jnp.exp(jnp.where(causal, diff, -jnp.inf))
        att = jnp.einsum('bthk,bshk,btshk->bhts', qc, kc, decay)
        o_intra = jnp.einsum('bhts,bshv->bthv', att, vc)
        b_last = b[:, -1]
        k_dec = kc * jnp.exp(b_last[:, None] - b)
        s_new = jnp.exp(b_last)[..., None] * s + jnp.einsum('bshk,bshv->bhkv', k_dec, vc)
        return s_new, o_inter + o_intra

    xs = (to_chunks(q), to_chunks(k), to_chunks(v), to_chunks(log_a))
    s_fin, o = lax.scan(step, s0.astype(jnp.float32), xs)
    o = jnp.moveaxis(o, 0, 1).reshape(bsz, n_chunks * c, GLA_HEADS, GLA_DV)[:, :t_len]
    return o, s_fin


def causal_dwconv(u, buf, w, b):
    ext = jnp.concatenate([buf.astype(u.dtype), u], axis=1)
    out = lax.conv_general_dilated(ext, w[:, None, :].astype(u.dtype), window_strides=(1,), padding='VALID',
                                   dimension_numbers=('NWC', 'WIO', 'NWC'), feature_group_count=u.shape[-1])
    return out + b, ext[:, -(CONV_WIDTH - 1):]


def hybrid_layer(x, p, s0, conv_buf, g_pre, w_in, w_a_up, b_a_up, g_gla, w_a_out,
                 w_dw, b_dw, g_ln, b_ln, w_b_out, w_o, w_pe, w_pg):
    bsz, t_len, _ = x.shape
    h = rmsnorm(x, g_pre)
    proj = h @ w_in
    q, k, v, z_a, r, glu_in, z_b, g_a, g_b = jnp.split(proj, SPLIT_POINTS, axis=-1)
    q = q.reshape(bsz, t_len, GLA_HEADS, GLA_DK) * (GLA_DK ** -0.5)
    k = k.reshape(bsz, t_len, GLA_HEADS, GLA_DK)
    v = v.reshape(bsz, t_len, GLA_HEADS, GLA_DV)
    log_a = jax.nn.log_sigmoid((r @ w_a_up + b_a_up).astype(jnp.float32)) / GLA_TAU
    log_a = log_a.reshape(bsz, t_len, GLA_HEADS, GLA_DK)
    o, s_new = gla_chunked(q, k, v, log_a, s0)
    o = rmsnorm(o, g_gla).reshape(bsz, t_len, GLA_VAL_DIM).astype(x.dtype)
    y_a = (o * jax.nn.silu(z_a)) @ w_a_out
    glu_a, glu_g = jnp.split(glu_in, 2, axis=-1)
    u = glu_a * jax.nn.sigmoid(glu_g)
    cv, new_buf = causal_dwconv(u, conv_buf, w_dw, b_dw)
    cv = jax.nn.silu(layernorm(cv, g_ln, b_ln))
    y_b = (cv * jax.nn.silu(z_b)) @ w_b_out
    merged = jax.nn.sigmoid(g_a) * y_a + jax.nn.sigmoid(g_b) * y_b
    x = x + merged @ w_o
    x = x + jax.nn.sigmoid(x @ w_pg) * (p @ w_pe)
    return x, s_new, new_buf


def setup_inputs(seed: int = 0) -> dict:
    key = jax.random.key(seed)
    ks = jax.random.split(key, 24)
    f32 = jnp.float32
    nrm = lambda kk, shape, scale: jax.random.normal(kk, shape, f32) * scale
    return {
        'x_prompt': nrm(ks[0], (BATCH, SEQ, D_MODEL), 1.0),
        'x_sample': nrm(ks[1], (DEC_BATCH, DEC_SEQ, D_MODEL), 1.0),
        'state_gla': nrm(ks[2], (DEPTH, DEC_BATCH, GLA_HEADS, GLA_DK, GLA_DV), 1.0),
        'state_conv': nrm(ks[3], (DEPTH, DEC_BATCH, CONV_WIDTH - 1, CONV_DIM), 0.5),
        'p_prompt': nrm(ks[4], (DEPTH, BATCH, SEQ, PLE_DIM), 1.0),
        'p_sample': nrm(ks[5], (DEPTH, DEC_BATCH, DEC_SEQ, PLE_DIM), 1.0),
        'g_pre': 1.0 + nrm(ks[6], (DEPTH, D_MODEL), 0.02),
        'w_in': nrm(ks[7], (DEPTH, D_MODEL, D_IN), D_MODEL ** -0.5),
        'w_a_up': nrm(ks[8], (DEPTH, GLA_RANK, GLA_KEY_DIM), GLA_RANK ** -0.5),
        'b_a_up': nrm(ks[9], (DEPTH, GLA_KEY_DIM), 0.1),
        'g_gla': 1.0 + nrm(ks[10], (DEPTH, GLA_DV), 0.02),
        'w_a_out': nrm(ks[11], (DEPTH, GLA_VAL_DIM, D_MODEL), GLA_VAL_DIM ** -0.5),
        'w_dw': nrm(ks[12], (DEPTH, CONV_WIDTH, CONV_DIM), CONV_WIDTH ** -0.5),
        'b_dw': nrm(ks[13], (DEPTH, CONV_DIM), 0.02),
        'g_ln': 1.0 + nrm(ks[14], (DEPTH, CONV_DIM), 0.02),
        'b_ln': nrm(ks[15], (DEPTH, CONV_DIM), 0.02),
        'w_b_out': nrm(ks[16], (DEPTH, CONV_DIM, D_MODEL), CONV_DIM ** -0.5),
        'w_o': nrm(ks[17], (DEPTH, D_MODEL, D_MODEL), D_MODEL ** -0.5),
        'w_pe': nrm(ks[18], (DEPTH, PLE_DIM, D_MODEL), PLE_DIM ** -0.5),
        'w_pg': nrm(ks[19], (DEPTH, D_MODEL, D_MODEL), D_MODEL ** -0.5),
        'g_final': 1.0 + nrm(ks[20], (D_MODEL,), 0.02),
    }


def reference(x_prompt, x_sample, state_gla, state_conv, p_prompt, p_sample, g_pre, w_in, w_a_up, b_a_up,
              g_gla, w_a_out, w_dw, b_dw, g_ln, b_ln, w_b_out, w_o, w_pe, w_pg, g_final):
    bp = x_prompt.shape[0]
    xp, xs = x_prompt, x_sample
    gla_p, conv_p, gla_s, conv_s = [], [], [], []
    for i in range(DEPTH):
        lw = (g_pre[i], w_in[i], w_a_up[i], b_a_up[i], g_gla[i], w_a_out[i], w_dw[i], b_dw[i],
              g_ln[i], b_ln[i], w_b_out[i], w_o[i], w_pe[i], w_pg[i])
        s0_p = jnp.zeros((bp, GLA_HEADS, GLA_DK, GLA_DV), jnp.float32)
        buf_p = jnp.zeros((bp, CONV_WIDTH - 1, CONV_DIM), xp.dtype)
        xp, sp, bfp = hybrid_layer(xp, p_prompt[i], s0_p, buf_p, *lw)
        xs, ss, bfs = hybrid_layer(xs, p_sample[i], state_gla[i], state_conv[i], *lw)
        gla_p.append(sp.astype(state_gla.dtype))
        conv_p.append(bfp.astype(state_conv.dtype))
        gla_s.append(ss.astype(state_gla.dtype))
        conv_s.append(bfs.astype(state_conv.dtype))
    y_prompt = rmsnorm(xp, g_final)
    y_sample = rmsnorm(xs, g_final)
    gla_state_prompt = jnp.stack(gla_p)
    conv_state_prompt = jnp.stack(conv_p)
    gla_state_sample = jnp.stack(gla_s)
    conv_state_sample = jnp.stack(conv_s)
    return (y_prompt, y_sample, gla_state_prompt, conv_state_prompt, gla_state_sample, conv_state_sample)
```

```python
import functools

import numpy as np
import jax
import jax.numpy as jnp
from jax import lax
from jax.experimental import pallas as pl
from jax.experimental.pallas import tpu as pltpu

F32 = jnp.float32
BF16 = jnp.bfloat16

D_MODEL = 1024
HEADS = 4
DK = 128
DV = 256
KEY_DIM = HEADS * DK
VAL_DIM = HEADS * DV
RANK = 16
TAU = 16.0
CHUNK = 64
SUB = 16
CONV_W = 31
HIST = CONV_W - 1
HIST_PAD = 32
PLE = 256
EPS = 1e-6
LANES = 128
VMEM_LIMIT = 56 * 1024 * 1024


def _dot(a, b):
    return jnp.dot(a, b, preferred_element_type=F32)


def _sigmoid(x):
    return 1.0 / (1.0 + jnp.exp(-x))


def _silu(x):
    return x * _sigmoid(x)


def _log_sigmoid(x):
    return -(jnp.maximum(-x, 0.0) + jnp.log1p(jnp.exp(-jnp.abs(x))))


def _rms(x, g):
    return x * lax.rsqrt(jnp.mean(x * x, axis=-1, keepdims=True) + EPS) * g


def _pre_kernel(x_ref, gpre_ref, wa_ref, wr_ref, wup_ref, bup_ref, wb_ref,
                q_ref, k_ref, la_ref, v_ref, sza_ref, u_ref, szb_ref, sga_ref, sgb_ref):
    h = _rms(x_ref[...], gpre_ref[...]).astype(BF16)
    q_ref[...] = _dot(h, wa_ref[:, 0:KEY_DIM]) * (DK ** -0.5)
    k_ref[...] = _dot(h, wa_ref[:, KEY_DIM:2 * KEY_DIM])
    v_ref[...] = _dot(h, wa_ref[:, 2 * KEY_DIM:2 * KEY_DIM + VAL_DIM]).astype(BF16)
    sza_ref[...] = _silu(_dot(h, wa_ref[:, 2 * KEY_DIM + VAL_DIM:]))
    r = _dot(h, wr_ref[...]).astype(BF16)
    la_ref[...] = _log_sigmoid(_dot(r, wup_ref[...]) + bup_ref[...]) * (1.0 / TAU)
    c = D_MODEL
    u_ref[...] = _dot(h, wb_ref[:, 0:c]) * _sigmoid(_dot(h, wb_ref[:, c:2 * c]))
    szb_ref[...] = _silu(_dot(h, wb_ref[:, 2 * c:3 * c]))
    sga_ref[...] = _sigmoid(_dot(h, wb_ref[:, 3 * c:4 * c]))
    sgb_ref[...] = _sigmoid(_dot(h, wb_ref[:, 4 * c:5 * c]))


def _whole(shape):
    return pl.BlockSpec(memory_space=pltpu.VMEM)


def _pre_call(x2d, gpre, wa, wr, wup, bup, wb, tm):
    n = x2d.shape[0]
    row = lambda w: pl.BlockSpec((tm, w), lambda i: (i, 0))
    outs = [(KEY_DIM, F32), (KEY_DIM, F32), (KEY_DIM, F32), (VAL_DIM, BF16), (VAL_DIM, F32),
            (D_MODEL, F32), (D_MODEL, F32), (D_MODEL, F32), (D_MODEL, F32)]
    return pl.pallas_call(
        _pre_kernel,
        grid=(n // tm,),
        in_specs=[row(D_MODEL)] + [_whole(None)] * 6,
        out_specs=[row(w) for w, _ in outs],
        out_shape=[jax.ShapeDtypeStruct((n, w), dt) for w, dt in outs],
        compiler_params=pltpu.CompilerParams(dimension_semantics=("arbitrary",),
                                             vmem_limit_bytes=VMEM_LIMIT),
        name="pre",
    )(x2d, gpre, wa, wr, wup, bup, wb)


def _gla_chunk(q, k, a, v, s_list, tri, emat, mcat_ref, n_keys):
    c = q.shape[0]
    n_sub = c // SUB
    a0 = a.astype(BF16)
    r1 = a - a0.astype(F32)
    a1 = r1.astype(BF16)
    a2 = (r1 - a1.astype(F32)).astype(BF16)
    b = _dot(tri, a0) + _dot(tri, a1) + _dot(tri, a2)

    att_off, q_dec, k_dec, d_last = [], [], [], []
    for h in range(HEADS):
        sl = slice(h * DK, (h + 1) * DK)
        qh, kh, bh = q[:, sl], k[:, sl], b[:, sl]
        bl = bh[c - 1:c, :]
        q_dec.append((qh * jnp.exp(bh)).astype(BF16))
        k_dec.append(kh * jnp.exp(bl - bh))
        d_last.append(jnp.exp(bl))
        rows = [jnp.zeros((SUB, c), F32)]
        for i in range(1, n_sub):
            lo = i * SUB
            br = bh[lo - 1:lo, :]
            qi = (qh[lo:lo + SUB] * jnp.exp(bh[lo:lo + SUB] - br)).astype(BF16)
            ki = (kh[:lo] * jnp.exp(br - bh[:lo])).astype(BF16)
            kfull = jnp.concatenate([ki, jnp.zeros((c - lo, DK), BF16)], axis=0)
            rows.append(lax.dot_general(qi, kfull, (((1,), (1,)), ((), ())),
                                        preferred_element_type=F32))
        att_off.append(jnp.concatenate(rows, axis=0) if n_sub > 1 else rows[0])
        for i in range(n_sub):
            lo = i * SUB
            qb, kb, bb = qh[lo:lo + SUB], kh[lo:lo + SUB], bh[lo:lo + SUB]
            for s in range(n_keys):
                dec = jnp.exp(jnp.minimum(bb - bb[s:s + 1, :], 0.0))
                m = qb * dec * kb[s:s + 1, :]
                mcat_ref[h * c + lo:h * c + lo + SUB, s * DK:(s + 1) * DK] = m.astype(BF16)

    rsum = _dot(mcat_ref[:, 0:n_keys * DK], emat[0:n_keys * DK, :])
    ti = lax.broadcasted_iota(jnp.int32, (c, c), 0)
    si = lax.broadcasted_iota(jnp.int32, (c, c), 1)
    diag_mask = (ti // SUB == si // SUB) & (si <= ti)

    o_parts, s_new = [], []
    for h in range(HEADS):
        vh = v[:, h * DV:(h + 1) * DV]
        att = att_off[h] + jnp.where(diag_mask, rsum[h * c:(h + 1) * c, 0:c], 0.0)
        lhs = jnp.concatenate([q_dec[h], att.astype(BF16)], axis=1)
        rhs = jnp.concatenate([s_list[h].astype(BF16), vh], axis=0)
        o_parts.append(_dot(lhs, rhs))
        dcol = jnp.broadcast_to(d_last[h], (DK, DK)).T
        kt = k_dec[h].T.astype(BF16)
        s_new.append(jnp.concatenate([dcol, dcol], axis=1) * s_list[h] + _dot(kt, vh))
    return jnp.concatenate(o_parts, axis=1), s_new


def _tri_const(c):
    return jnp.asarray(np.tril(np.ones((c, c), np.float32)), BF16)


def _emat_const():
    e = np.zeros((SUB * DK, LANES), np.float32)
    rows = np.arange(SUB * DK)
    for l in range(LANES):
        e[rows[rows // DK == l % SUB], l] = 1.0
    return jnp.asarray(e, BF16)


def _conv_block(ext_ref, wdw_ref, bdw_ref, out_ref, n_rows, base):
    rb = min(n_rows, 128)
    for cb in range(D_MODEL // LANES):
        cs = slice(cb * LANES, (cb + 1) * LANES)
        w = wdw_ref[:, cs]
        for r0 in range(0, n_rows, rb):
            acc = jnp.broadcast_to(bdw_ref[:, cs], (rb, LANES))
            for j in range(CONV_W):
                acc = acc + w[j:j + 1, :] * ext_ref[base + r0 + j:base + r0 + j + rb, cs]
            out_ref[r0:r0 + rb, cs] = acc


def _mix_prompt_kernel(q_ref, k_ref, la_ref, v_ref, u_ref, tri_ref, emat_ref, wdw_ref, bdw_ref,
                       o_ref, cv_ref, s_ref, cs_ref, ext_ref, mcat_ref, *, tt):
    t = pl.program_id(1)

    @pl.when(t == 0)
    def _():
        s_ref[...] = jnp.zeros(s_ref.shape, F32)
        ext_ref[0:HIST_PAD, :] = jnp.zeros((HIST_PAD, D_MODEL), F32)

    def chunk(ci, carry):
        r0 = pl.multiple_of(ci * CHUNK, CHUNK)
        rows = pl.ds(r0, CHUNK)
        s_list = [s_ref[0, h] for h in range(HEADS)]
        o, s_new = _gla_chunk(q_ref[0, rows, :], k_ref[0, rows, :], la_ref[0, rows, :], v_ref[0, rows, :],
                              s_list, tri_ref[...], emat_ref, mcat_ref, SUB)
        o_ref[0, rows, :] = o
        for h in range(HEADS):
            s_ref[0, h] = s_new[h]
        return carry

    lax.fori_loop(0, tt // CHUNK, chunk, 0)

    ext_ref[HIST_PAD:HIST_PAD + tt, :] = u_ref[0]
    _conv_block(ext_ref, wdw_ref, bdw_ref, cv_ref.at[0], tt, HIST_PAD - HIST)
    cs_ref[0] = ext_ref[tt + HIST_PAD - HIST:tt + HIST_PAD, :]
    ext_ref[0:HIST_PAD, :] = ext_ref[tt:tt + HIST_PAD, :]


def _mix_prompt_call(q, k, la, v, u, tri, emat, wdw, bdw, tt):
    bsz, t_len, _ = q.shape
    tok = lambda w: pl.BlockSpec((1, tt, w), lambda b, t: (b, t, 0))
    return pl.pallas_call(
        functools.partial(_mix_prompt_kernel, tt=tt),
        grid=(bsz, t_len // tt),
        in_specs=[tok(KEY_DIM), tok(KEY_DIM), tok(KEY_DIM), tok(VAL_DIM), tok(D_MODEL)] + [_whole(None)] * 4,
        out_specs=[tok(VAL_DIM), tok(D_MODEL),
                   pl.BlockSpec((1, HEADS, DK, DV), lambda b, t: (b, 0, 0, 0)),
                   pl.BlockSpec((1, HIST, D_MODEL), lambda b, t: (b, 0, 0))],
        out_shape=[jax.ShapeDtypeStruct((bsz, t_len, VAL_DIM), F32),
                   jax.ShapeDtypeStruct((bsz, t_len, D_MODEL), F32),
                   jax.ShapeDtypeStruct((bsz, HEADS, DK, DV), F32),
                   jax.ShapeDtypeStruct((bsz, HIST, D_MODEL), F32)],
        scratch_shapes=[pltpu.VMEM((HIST_PAD + tt, D_MODEL), F32),
                        pltpu.VMEM((HEADS * CHUNK, SUB * DK), BF16)],
        compiler_params=pltpu.CompilerParams(dimension_semantics=("arbitrary", "arbitrary"),
                                             vmem_limit_bytes=VMEM_LIMIT),
        name="mix_prompt",
    )(q, k, la, v, u, tri, emat, wdw, bdw)


def _mix_sample_kernel(q_ref, k_ref, la_ref, v_ref, u_ref, s0_ref, c0_ref, tri_ref, emat_ref, wdw_ref, bdw_ref,
                       o_ref, cv_ref, s_ref, cs_ref, ext_ref, mcat_ref, cvt_ref, *, nb, t_len):
    pad = SUB - t_len

    def seq(n, carry):
        def padf(x):
            xf = x.astype(F32)
            return jnp.concatenate([xf, jnp.zeros((pad, x.shape[1]), F32)], axis=0).astype(x.dtype)

        s_list = [s0_ref[n, h] for h in range(HEADS)]
        o, s_new = _gla_chunk(padf(q_ref[n]), padf(k_ref[n]), padf(la_ref[n]), padf(v_ref[n]),
                              s_list, tri_ref[...], emat_ref, mcat_ref, t_len)
        o_ref[n] = o[0:t_len]
        for h in range(HEADS):
            s_ref[n, h] = s_new[h]
        ext_ref[0:HIST, :] = c0_ref[n]
        ext_ref[HIST:HIST + t_len, :] = u_ref[n]
        _conv_block(ext_ref, wdw_ref, bdw_ref, cvt_ref, t_len, 0)
        cv_ref[n] = cvt_ref[...]
        cs_ref[n] = ext_ref[t_len:t_len + HIST, :]
        return carry

    lax.fori_loop(0, nb, seq, 0)


def _mix_sample_call(q, k, la, v, u, s0, c0, tri, emat, wdw, bdw, nb):
    bsz, t_len, _ = q.shape
    tok = lambda w: pl.BlockSpec((nb, t_len, w), lambda i: (i, 0, 0))
    st = pl.BlockSpec((nb, HEADS, DK, DV), lambda i: (i, 0, 0, 0))
    cst = pl.BlockSpec((nb, HIST, D_MODEL), lambda i: (i, 0, 0))
    return pl.pallas_call(
        functools.partial(_mix_sample_kernel, nb=nb, t_len=t_len),
        grid=(bsz // nb,),
        in_specs=[tok(KEY_DIM), tok(KEY_DIM), tok(KEY_DIM), tok(VAL_DIM), tok(D_MODEL), st, cst] + [_whole(None)] * 4,
        out_specs=[tok(VAL_DIM), tok(D_MODEL), st, cst],
        out_shape=[jax.ShapeDtypeStruct((bsz, t_len, VAL_DIM), F32),
                   jax.ShapeDtypeStruct((bsz, t_len, D_MODEL), F32),
                   jax.ShapeDtypeStruct((bsz, HEADS, DK, DV), F32),
                   jax.ShapeDtypeStruct((bsz, HIST, D_MODEL), F32)],
        scratch_shapes=[pltpu.VMEM((HIST + t_len + 6, D_MODEL), F32),
                        pltpu.VMEM((HEADS * SUB, SUB * DK), BF16),
                        pltpu.VMEM((t_len, D_MODEL), F32)],
        compiler_params=pltpu.CompilerParams(dimension_semantics=("arbitrary",),
                                             vmem_limit_bytes=VMEM_LIMIT),
        name="mix_sample",
    )(q, k, la, v, u, s0, c0, tri, emat, wdw, bdw)


def _post_kernel(x_ref, o_ref, cv_ref, sza_ref, szb_ref, sga_ref, sgb_ref, p_ref,
                 ggla_ref, gln_ref, bln_ref, wao_ref, wbo_ref, wo_ref, wpg_ref, wpe_ref, gfin_ref, y_ref):
    o = o_ref[...]
    g = ggla_ref[...]
    on = jnp.concatenate([_rms(o[:, h * DV:(h + 1) * DV], g) for h in range(HEADS)], axis=1)
    y_a = _dot((on * sza_ref[...]).astype(BF16), wao_ref[...])
    cv = cv_ref[...]
    mu = jnp.mean(cv, axis=-1, keepdims=True)
    xc = cv - mu
    ln = xc * lax.rsqrt(jnp.mean(xc * xc, axis=-1, keepdims=True) + EPS) * gln_ref[...] + bln_ref[...]
    y_b = _dot((_silu(ln) * szb_ref[...]).astype(BF16), wbo_ref[...])
    merged = sga_ref[...] * y_a + sgb_ref[...] * y_b
    x1 = x_ref[...] + _dot(merged.astype(BF16), wo_ref[...])
    x2 = x1 + _sigmoid(_dot(x1.astype(BF16), wpg_ref[...])) * _dot(p_ref[...].astype(BF16), wpe_ref[...])
    y_ref[...] = _rms(x2, gfin_ref[...])


def _post_call(x2d, o, cv, sza, szb, sga, sgb, p2d, ggla, gln, bln, wao, wbo, wo, wpg, wpe, gfin, tm):
    n = x2d.shape[0]
    row = lambda w: pl.BlockSpec((tm, w), lambda i: (i, 0))
    return pl.pallas_call(
        _post_kernel,
        grid=(n // tm,),
        in_specs=[row(D_MODEL)] * 7 + [row(PLE)] + [_whole(None)] * 9,
        out_specs=row(D_MODEL),
        out_shape=jax.ShapeDtypeStruct((n, D_MODEL), F32),
        compiler_params=pltpu.CompilerParams(dimension_semantics=("arbitrary",),
                                             vmem_limit_bytes=VMEM_LIMIT),
        name="post",
    )(x2d, o, cv, sza, szb, sga, sgb, p2d, ggla, gln, bln, wao, wbo, wo, wpg, wpe, gfin)


def _layer(x, p, s0, c0, w, tm, tt, nb):
    bsz, t_len, _ = x.shape
    x2d = x.reshape(bsz * t_len, D_MODEL)
    q, k, la, v, sza, u, szb, sga, sgb = _pre_call(x2d, w["gpre"], w["wa"], w["wr"], w["wup"], w["bup"], w["wb"], tm)
    r3 = lambda z: z.reshape(bsz, t_len, z.shape[-1])
    if s0 is None:
        o, cv, s_new, c_new = _mix_prompt_call(r3(q), r3(k), r3(la), r3(v), r3(u), _tri_const(CHUNK), w["emat"],
                                               w["wdw"], w["bdw"], tt)
    else:
        o, cv, s_new, c_new = _mix_sample_call(r3(q), r3(k), r3(la), r3(v), r3(u), s0, c0, _tri_const(SUB),
                                               w["emat"], w["wdw"], w["bdw"], nb)
    y = _post_call(x2d, o.reshape(-1, VAL_DIM), cv.reshape(-1, D_MODEL), sza, szb, sga, sgb,
                   p.reshape(bsz * t_len, PLE), w["ggla"], w["gln"], w["bln"], w["wao"], w["wbo"], w["wo"],
                   w["wpg"], w["wpe"], w["gfin"], tm)
    return y.reshape(bsz, t_len, D_MODEL), s_new, c_new


def kernel(x_prompt, x_sample, state_gla, state_conv, p_prompt, p_sample, g_pre, w_in, w_a_up, b_a_up, g_gla, w_a_out, w_dw, b_dw, g_ln, b_ln, w_b_out, w_o, w_pe, w_pg, g_final):
    depth = w_in.shape[0]
    assert depth == 1, "one trunk layer"
    i = 0
    n_a = 2 * KEY_DIM + 2 * VAL_DIM
    row = lambda z: z.reshape(1, -1)
    w = {
        "gpre": row(g_pre[i]),
        "wa": w_in[i][:, :n_a].astype(BF16),
        "wr": jnp.pad(w_in[i][:, n_a:n_a + RANK], ((0, 0), (0, LANES - RANK))).astype(BF16),
        "wup": jnp.pad(w_a_up[i], ((0, LANES - RANK), (0, 0))).astype(BF16),
        "bup": row(b_a_up[i]),
        "wb": w_in[i][:, n_a + RANK:].astype(BF16),
        "emat": _emat_const(),
        "wdw": w_dw[i], "bdw": row(b_dw[i]),
        "ggla": row(g_gla[i]), "gln": row(g_ln[i]), "bln": row(b_ln[i]),
        "wao": w_a_out[i].astype(BF16), "wbo": w_b_out[i].astype(BF16), "wo": w_o[i].astype(BF16),
        "wpg": w_pg[i].astype(BF16), "wpe": w_pe[i].astype(BF16), "gfin": row(g_final),
    }
    yp, sp, cp = _layer(x_prompt, p_prompt[i], None, None, w, tm=256, tt=256, nb=None)
    ys, ss, cs = _layer(x_sample, p_sample[i], state_gla[i], state_conv[i], w, tm=256, tt=None, nb=8)
    return (yp, ys, sp[None], cp[None], ss[None], cs[None])
```

```python
import functools

import numpy as np
import jax
import jax.numpy as jnp
from jax import lax
from jax.experimental import pallas as pl
from jax.experimental.pallas import tpu as pltpu

F32 = jnp.float32
BF16 = jnp.bfloat16

D_MODEL = 1024
HEADS = 4
DK = 128
DV = 256
KEY_DIM = HEADS * DK
VAL_DIM = HEADS * DV
RANK = 16
TAU = 16.0
CHUNK = 64
SUB = 16
CONV_W = 31
HIST = CONV_W - 1
PLE = 256
EPS = 1e-6
LOG2E = 1.4426950408889634
LANES = 128
SUBLANES = 8
N_LB = D_MODEL // LANES
VMEM_LIMIT = 58 * 1024 * 1024


def _tiles():
    return dict(tt=256, tm=256, nb=8)


def _dot(a, b):
    return jnp.dot(a, b, preferred_element_type=F32)


def _sigmoid(x):
    return 1.0 / (1.0 + jnp.exp(-x))


def _silu(x):
    return x * _sigmoid(x)


def _log_sigmoid(x):
    return -(jnp.maximum(-x, 0.0) + jnp.log1p(jnp.exp(-jnp.abs(x))))


def _rms(x, g):
    return x * lax.rsqrt(jnp.mean(x * x, axis=-1, keepdims=True) + EPS) * g


def _qkv_la(h, wa_ref, wr_ref, wup_ref, bup_ref):
    q = _dot(h, wa_ref[:, 0:KEY_DIM]) * (DK ** -0.5)
    k = _dot(h, wa_ref[:, KEY_DIM:2 * KEY_DIM])
    v = _dot(h, wa_ref[:, 2 * KEY_DIM:2 * KEY_DIM + VAL_DIM]).astype(BF16)
    r = _dot(h, wr_ref[...]).astype(BF16)
    la = _log_sigmoid(_dot(r, wup_ref[...]) + bup_ref[...]) * (LOG2E / TAU)
    return q, k, la, v


def _z_a(h, wa_ref):
    return _silu(_dot(h, wa_ref[:, 2 * KEY_DIM + VAL_DIM:]))


def _glu(h, wb_ref):
    c = D_MODEL
    return _dot(h, wb_ref[:, 0:c]) * _sigmoid(_dot(h, wb_ref[:, c:2 * c]))


def _branch_out(h, x, o, cv, p, sza, wb_ref, ggla_ref, gln_ref, bln_ref, wao_ref, wbo_ref, wo_ref, wpg_ref,
                wpe_ref, gfin_ref):
    c = D_MODEL
    g = ggla_ref[...]
    on = jnp.concatenate([_rms(o[:, hd * DV:(hd + 1) * DV], g) for hd in range(HEADS)], axis=1)
    y_a = _dot((on * sza).astype(BF16), wao_ref[...])
    mu = jnp.mean(cv, axis=-1, keepdims=True)
    xc = cv - mu
    ln = xc * lax.rsqrt(jnp.mean(xc * xc, axis=-1, keepdims=True) + EPS) * gln_ref[...] + bln_ref[...]
    szb = _silu(_dot(h, wb_ref[:, 2 * c:3 * c]))
    y_b = _dot((_silu(ln) * szb).astype(BF16), wbo_ref[...])
    merged = (_sigmoid(_dot(h, wb_ref[:, 3 * c:4 * c])) * y_a
              + _sigmoid(_dot(h, wb_ref[:, 4 * c:5 * c])) * y_b)
    x1 = x + _dot(merged.astype(BF16), wo_ref[...])
    x2 = x1 + _sigmoid(_dot(x1.astype(BF16), wpg_ref[...])) * _dot(p.astype(BF16), wpe_ref[...])
    return _rms(x2, gfin_ref[...])


def _gla_chunk(q, k, a, v, s_list, tri, emat, mcat_ref, n_keys):
    c = q.shape[0]
    n_sub = c // SUB
    half = SUB // 2
    a0 = a.astype(BF16)
    r1 = a - a0.astype(F32)
    a1 = r1.astype(BF16)
    a2 = (r1 - a1.astype(F32)).astype(BF16)
    b = _dot(tri, a0) + _dot(tri, a1) + _dot(tri, a2)

    att_off, q_dec, k_dec, d_last = [], [], [], []
    for h in range(HEADS):
        sl = slice(h * DK, (h + 1) * DK)
        qh, kh, bh = q[:, sl], k[:, sl], b[:, sl]
        bl = bh[c - 1:c, :]
        q_dec.append((qh * jnp.exp2(bh)).astype(BF16))
        k_dec.append(kh * jnp.exp2(bl - bh))
        d_last.append(jnp.exp2(bl))
        rows = [jnp.zeros((SUB, c), F32)]
        for i in range(1, n_sub):
            lo = i * SUB
            br = bh[lo - 1:lo, :]
            qi = (qh[lo:lo + SUB] * jnp.exp2(bh[lo:lo + SUB] - br)).astype(BF16)
            ki = (kh[:lo] * jnp.exp2(br - bh[:lo])).astype(BF16)
            kfull = jnp.concatenate([ki, jnp.zeros((c - lo, DK), BF16)], axis=0)
            rows.append(lax.dot_general(qi, kfull, (((1,), (1,)), ((), ())),
                                        preferred_element_type=F32))
        att_off.append(jnp.concatenate(rows, axis=0) if n_sub > 1 else rows[0])
        for i in range(n_sub):
            lo = i * SUB
            for s in range(n_keys):
                top = 0 if s < half else half
                qb, bb = qh[lo + top:lo + SUB], bh[lo + top:lo + SUB]
                dec = jnp.exp2(jnp.minimum(bb - bh[lo + s:lo + s + 1, :], 0.0))
                m = qb * dec * kh[lo + s:lo + s + 1, :]
                if top:
                    m = jnp.concatenate([jnp.zeros((top, DK), F32), m], axis=0)
                mcat_ref[h * c + lo:h * c + lo + SUB, s * DK:(s + 1) * DK] = m.astype(BF16)

    rsum = _dot(mcat_ref[:, 0:n_keys * DK], emat[0:n_keys * DK, :])
    ti = lax.broadcasted_iota(jnp.int32, (c, c), 0)
    si = lax.broadcasted_iota(jnp.int32, (c, c), 1)
    diag_mask = (ti // SUB == si // SUB) & (si <= ti)

    o_parts, s_new = [], []
    for h in range(HEADS):
        vh = v[:, h * DV:(h + 1) * DV]
        att = att_off[h] + jnp.where(diag_mask, rsum[h * c:(h + 1) * c, 0:c], 0.0)
        lhs = jnp.concatenate([q_dec[h], att.astype(BF16)], axis=1)
        rhs = jnp.concatenate([s_list[h].astype(BF16), vh], axis=0)
        o_parts.append(_dot(lhs, rhs))
        dcol = jnp.broadcast_to(d_last[h], (DK, DK)).T
        kt = k_dec[h].T.astype(BF16)
        s_new.append(jnp.concatenate([dcol, dcol], axis=1) * s_list[h] + _dot(kt, vh))
    return jnp.concatenate(o_parts, axis=1), s_new


def _tri_const(c):
    return jnp.asarray(np.tril(np.ones((c, c), np.float32)), BF16)


def _emat_const():
    e = np.zeros((SUB * DK, LANES), np.float32)
    rows = np.arange(SUB * DK)
    for l in range(LANES):
        e[rows[rows // DK == l % SUB], l] = 1.0
    return jnp.asarray(e, BF16)


def _conv_tile(u, xs_ref, hist_ref, cvs_ref, wdwb_ref, bdwb_ref, tt):
    seg = tt // SUBLANES
    x0 = HIST * SUBLANES
    sub7 = lax.broadcasted_iota(jnp.int32, (SUBLANES, LANES), 0) == SUBLANES - 1
    grp = SUBLANES
    for cb in range(N_LB):
        cs = slice(cb * LANES, (cb + 1) * LANES)
        for s in range(SUBLANES):
            for a0 in range(0, seg, SUBLANES):
                val = u[seg * s + a0:seg * s + a0 + SUBLANES, cs]
                xs_ref[cb, pl.ds(x0 + SUBLANES * a0 + s, SUBLANES, stride=SUBLANES), :] = val
        for a in range(seg - HIST, seg):
            e = a - (seg - HIST)
            new = xs_ref[cb, x0 + SUBLANES * a:x0 + SUBLANES * (a + 1), :]
            old = hist_ref[cb, SUBLANES * e:SUBLANES * (e + 1), :]
            xs_ref[cb, SUBLANES * e:SUBLANES * (e + 1), :] = pltpu.roll(jnp.where(sub7, old, new), 1, axis=0)
            hist_ref[cb, SUBLANES * e:SUBLANES * (e + 1), :] = new
        w = [wdwb_ref[SUBLANES * j:SUBLANES * (j + 1), cs] for j in range(CONV_W)]
        for g0 in range(0, seg, grp):
            acc = [bdwb_ref[:, cs] for _ in range(grp)]
            for d in range(g0, g0 + grp + HIST):
                x = xs_ref[cb, SUBLANES * d:SUBLANES * (d + 1), :]
                for i in range(grp):
                    j = d - (g0 + i)
                    if 0 <= j < CONV_W:
                        acc[i] = acc[i] + w[j] * x
            for i in range(grp):
                cvs_ref[cb, pl.ds(g0 + i, SUBLANES, stride=seg), :] = acc[i]


def _prompt_kernel(x_ref, p_ref, gpre_ref, wa_ref, wr_ref, wup_ref, bup_ref, wb_ref, tri_ref, emat_ref,
                   wdwb_ref, bdwb_ref, ggla_ref, gln_ref, bln_ref, wao_ref, wbo_ref, wo_ref, wpg_ref, wpe_ref,
                   gfin_ref,
                   y_ref, s_ref, cs_ref,
                   q_ref, k_ref, la_ref, v_ref, o_ref, mcat_ref, xs_ref, hist_ref, cvs_ref, *, tt):
    t = pl.program_id(1)

    @pl.when(t == 0)
    def _():
        s_ref[...] = jnp.zeros(s_ref.shape, F32)
        hist_ref[...] = jnp.zeros(hist_ref.shape, F32)

    x = x_ref[0]
    h = _rms(x, gpre_ref[...]).astype(BF16)
    q, k, la, v = _qkv_la(h, wa_ref, wr_ref, wup_ref, bup_ref)
    q_ref[...] = q
    k_ref[...] = k
    la_ref[...] = la
    v_ref[...] = v

    def chunk(ci, carry):
        r0 = pl.multiple_of(ci * CHUNK, CHUNK)
        rows = pl.ds(r0, CHUNK)
        s_list = [s_ref[0, hd] for hd in range(HEADS)]
        o, s_new = _gla_chunk(q_ref[rows, :], k_ref[rows, :], la_ref[rows, :], v_ref[rows, :],
                              s_list, tri_ref[...], emat_ref, mcat_ref, SUB)
        o_ref[rows, :] = o
        for hd in range(HEADS):
            s_ref[0, hd] = s_new[hd]
        return carry

    lax.fori_loop(0, tt // CHUNK, chunk, 0)

    u = _glu(h, wb_ref)
    cs_ref[0] = u[tt - HIST:tt, :]
    _conv_tile(u, xs_ref, hist_ref, cvs_ref, wdwb_ref, bdwb_ref, tt)
    cv = jnp.concatenate([cvs_ref[cb] for cb in range(N_LB)], axis=1)
    y_ref[0] = _branch_out(h, x, o_ref[...], cv, p_ref[0], _z_a(h, wa_ref), wb_ref, ggla_ref, gln_ref, bln_ref,
                           wao_ref, wbo_ref, wo_ref, wpg_ref, wpe_ref, gfin_ref)


def _whole():
    return pl.BlockSpec(memory_space=pltpu.VMEM)


def _prompt_call(x, p, w, tt):
    bsz, t_len, _ = x.shape
    seg = tt // SUBLANES
    assert t_len % tt == 0 and tt % CHUNK == 0 and seg >= HIST and seg % SUBLANES == 0
    tok = lambda wd: pl.BlockSpec((1, tt, wd), lambda b, t: (b, t, 0))
    names = ["gpre", "wa", "wr", "wup", "bup", "wb", "tri", "emat", "wdwb", "bdwb", "ggla", "gln", "bln",
             "wao", "wbo", "wo", "wpg", "wpe", "gfin"]
    return pl.pallas_call(
        functools.partial(_prompt_kernel, tt=tt),
        grid=(bsz, t_len // tt),
        in_specs=[tok(D_MODEL), tok(PLE)] + [_whole()] * len(names),
        out_specs=[tok(D_MODEL),
                   pl.BlockSpec((1, HEADS, DK, DV), lambda b, t: (b, 0, 0, 0)),
                   pl.BlockSpec((1, HIST, D_MODEL), lambda b, t: (b, 0, 0))],
        out_shape=[jax.ShapeDtypeStruct((bsz, t_len, D_MODEL), F32),
                   jax.ShapeDtypeStruct((bsz, HEADS, DK, DV), F32),
                   jax.ShapeDtypeStruct((bsz, HIST, D_MODEL), F32)],
        scratch_shapes=[pltpu.VMEM((tt, KEY_DIM), F32), pltpu.VMEM((tt, KEY_DIM), F32),
                        pltpu.VMEM((tt, KEY_DIM), F32), pltpu.VMEM((tt, VAL_DIM), BF16),
                        pltpu.VMEM((tt, VAL_DIM), F32),
                        pltpu.VMEM((HEADS * CHUNK, SUB * DK), BF16),
                        pltpu.VMEM((N_LB, (HIST + seg) * SUBLANES, LANES), F32),
                        pltpu.VMEM((N_LB, HIST * SUBLANES, LANES), F32),
                        pltpu.VMEM((N_LB, tt, LANES), F32)],
        compiler_params=pltpu.CompilerParams(dimension_semantics=("arbitrary", "arbitrary"),
                                             vmem_limit_bytes=VMEM_LIMIT),
        name="prompt_layer",
    )(x, p, *[w[n] for n in names])


def _pre_kernel(x_ref, gpre_ref, wa_ref, wr_ref, wup_ref, bup_ref, wb_ref,
                q_ref, k_ref, la_ref, v_ref, u_ref):
    h = _rms(x_ref[...], gpre_ref[...]).astype(BF16)
    q_ref[...], k_ref[...], la_ref[...], v_ref[...] = _qkv_la(h, wa_ref, wr_ref, wup_ref, bup_ref)
    u_ref[...] = _glu(h, wb_ref)


def _pre_call(x2d, w, tm):
    n = x2d.shape[0]
    row = lambda wd: pl.BlockSpec((tm, wd), lambda i: (i, 0))
    outs = [(KEY_DIM, F32), (KEY_DIM, F32), (KEY_DIM, F32), (VAL_DIM, BF16), (D_MODEL, F32)]
    names = ["gpre", "wa", "wr", "wup", "bup", "wb"]
    return pl.pallas_call(
        _pre_kernel,
        grid=(n // tm,),
        in_specs=[row(D_MODEL)] + [_whole()] * len(names),
        out_specs=[row(wd) for wd, _ in outs],
        out_shape=[jax.ShapeDtypeStruct((n, wd), dt) for wd, dt in outs],
        compiler_params=pltpu.CompilerParams(dimension_semantics=("arbitrary",),
                                             vmem_limit_bytes=VMEM_LIMIT),
        name="sample_pre",
    )(x2d, *[w[n_] for n_ in names])


def _conv_short(ext_ref, wdw_ref, bdw_ref, out_ref, n_rows):
    for cb in range(N_LB):
        cs = slice(cb * LANES, (cb + 1) * LANES)
        w = wdw_ref[:, cs]
        acc = jnp.broadcast_to(bdw_ref[:, cs], (n_rows, LANES))
        for j in range(CONV_W):
            acc = acc + w[j:j + 1, :] * ext_ref[j:j + n_rows, cs]
        out_ref[0:n_rows, cs] = acc


def _mix_sample_kernel(q_ref, k_ref, la_ref, v_ref, u_ref, s0_ref, c0_ref, tri_ref, emat_ref, wdw_ref, bdw_ref,
                       o_ref, cv_ref, s_ref, cs_ref, ext_ref, mcat_ref, cvt_ref, *, nb, t_len):
    pad = SUB - t_len

    def seq(n, carry):
        def padf(z):
            zf = z.astype(F32)
            return jnp.concatenate([zf, jnp.zeros((pad, z.shape[1]), F32)], axis=0).astype(z.dtype)

        s_list = [s0_ref[n, hd] for hd in range(HEADS)]
        o, s_new = _gla_chunk(padf(q_ref[n]), padf(k_ref[n]), padf(la_ref[n]), padf(v_ref[n]),
                              s_list, tri_ref[...], emat_ref, mcat_ref, t_len)
        o_ref[n] = o[0:t_len]
        for hd in range(HEADS):
            s_ref[n, hd] = s_new[hd]
        ext_ref[0:HIST, :] = c0_ref[n]
        ext_ref[HIST:HIST + t_len, :] = u_ref[n]
        _conv_short(ext_ref, wdw_ref, bdw_ref, cvt_ref, t_len)
        cv_ref[n] = cvt_ref[...]
        cs_ref[n] = ext_ref[t_len:t_len + HIST, :]
        return carry

    lax.fori_loop(0, nb, seq, 0)


def _mix_sample_call(q, k, la, v, u, s0, c0, w, nb):
    bsz, t_len, _ = q.shape
    assert bsz % nb == 0 and t_len <= SUB // 2
    tok = lambda wd: pl.BlockSpec((nb, t_len, wd), lambda i: (i, 0, 0))
    st = pl.BlockSpec((nb, HEADS, DK, DV), lambda i: (i, 0, 0, 0))
    cst = pl.BlockSpec((nb, HIST, D_MODEL), lambda i: (i, 0, 0))
    return pl.pallas_call(
        functools.partial(_mix_sample_kernel, nb=nb, t_len=t_len),
        grid=(bsz // nb,),
        in_specs=[tok(KEY_DIM), tok(KEY_DIM), tok(KEY_DIM), tok(VAL_DIM), tok(D_MODEL), st, cst] + [_whole()] * 4,
        out_specs=[tok(VAL_DIM), tok(D_MODEL), st, cst],
        out_shape=[jax.ShapeDtypeStruct((bsz, t_len, VAL_DIM), F32),
                   jax.ShapeDtypeStruct((bsz, t_len, D_MODEL), F32),
                   jax.ShapeDtypeStruct((bsz, HEADS, DK, DV), F32),
                   jax.ShapeDtypeStruct((bsz, HIST, D_MODEL), F32)],
        scratch_shapes=[pltpu.VMEM((HIST + t_len + 6, D_MODEL), F32),
                        pltpu.VMEM((HEADS * SUB, SUB * DK), BF16),
                        pltpu.VMEM((t_len, D_MODEL), F32)],
        compiler_params=pltpu.CompilerParams(dimension_semantics=("arbitrary",),
                                             vmem_limit_bytes=VMEM_LIMIT),
        name="sample_mix",
    )(q, k, la, v, u, s0, c0, _tri_const(SUB), w["emat"], w["wdw"], w["bdw"])


def _post_kernel(x_ref, o_ref, cv_ref, p_ref, gpre_ref, wa_ref, wb_ref,
                 ggla_ref, gln_ref, bln_ref, wao_ref, wbo_ref, wo_ref, wpg_ref, wpe_ref, gfin_ref, y_ref):
    x = x_ref[...]
    h = _rms(x, gpre_ref[...]).astype(BF16)
    y_ref[...] = _branch_out(h, x, o_ref[...], cv_ref[...], p_ref[...], _z_a(h, wa_ref), wb_ref, ggla_ref,
                             gln_ref, bln_ref, wao_ref, wbo_ref, wo_ref, wpg_ref, wpe_ref, gfin_ref)


def _post_call(x2d, o, cv, p2d, w, tm):
    n = x2d.shape[0]
    row = lambda wd: pl.BlockSpec((tm, wd), lambda i: (i, 0))
    names = ["gpre", "wa", "wb", "ggla", "gln", "bln", "wao", "wbo", "wo", "wpg", "wpe", "gfin"]
    return pl.pallas_call(
        _post_kernel,
        grid=(n // tm,),
        in_specs=[row(D_MODEL)] * 3 + [row(PLE)] + [_whole()] * len(names),
        out_specs=row(D_MODEL),
        out_shape=jax.ShapeDtypeStruct((n, D_MODEL), F32),
        compiler_params=pltpu.CompilerParams(dimension_semantics=("arbitrary",),
                                             vmem_limit_bytes=VMEM_LIMIT),
        name="sample_post",
    )(x2d, o, cv, p2d, *[w[n_] for n_ in names])


def _sample_layer(x, p, s0, c0, w, tm, nb):
    bsz, t_len, _ = x.shape
    x2d = x.reshape(bsz * t_len, D_MODEL)
    q, k, la, v, u = _pre_call(x2d, w, tm)
    r3 = lambda z: z.reshape(bsz, t_len, z.shape[-1])
    o, cv, s_new, c_new = _mix_sample_call(r3(q), r3(k), r3(la), r3(v), r3(u), s0, c0, w, nb)
    y = _post_call(x2d, o.reshape(-1, VAL_DIM), cv.reshape(-1, D_MODEL), p.reshape(bsz * t_len, PLE), w, tm)
    return y.reshape(bsz, t_len, D_MODEL), s_new, c_new


def kernel(x_prompt, x_sample, state_gla, state_conv, p_prompt, p_sample, g_pre, w_in, w_a_up, b_a_up, g_gla, w_a_out, w_dw, b_dw, g_ln, b_ln, w_b_out, w_o, w_pe, w_pg, g_final):
    depth = w_in.shape[0]
    assert depth == 1, "one trunk layer"
    i = 0
    tl = _tiles()
    n_a = 2 * KEY_DIM + 2 * VAL_DIM
    row = lambda z: z.reshape(1, -1)
    w = {
        "gpre": row(g_pre[i]),
        "wa": w_in[i][:, :n_a].astype(BF16),
        "wr": jnp.pad(w_in[i][:, n_a:n_a + RANK], ((0, 0), (0, LANES - RANK))).astype(BF16),
        "wup": jnp.pad(w_a_up[i], ((0, LANES - RANK), (0, 0))).astype(BF16),
        "bup": row(b_a_up[i]),
        "wb": w_in[i][:, n_a + RANK:].astype(BF16),
        "tri": _tri_const(CHUNK), "emat": _emat_const(),
        "wdw": w_dw[i], "bdw": row(b_dw[i]),
        "wdwb": jnp.repeat(w_dw[i], SUBLANES, axis=0),
        "bdwb": jnp.broadcast_to(row(b_dw[i]), (SUBLANES, D_MODEL)),
        "ggla": row(g_gla[i]), "gln": row(g_ln[i]), "bln": row(b_ln[i]),
        "wao": w_a_out[i].astype(BF16), "wbo": w_b_out[i].astype(BF16), "wo": w_o[i].astype(BF16),
        "wpg": w_pg[i].astype(BF16), "wpe": w_pe[i].astype(BF16), "gfin": row(g_final),
    }
    yp, sp, cp = _prompt_call(x_prompt, p_prompt[i], w, tl["tt"])
    ys, ss, cs = _sample_layer(x_sample, p_sample[i], state_gla[i], state_conv[i], w, tl["tm"], tl["nb"])
    return (yp, ys, sp[None], cp[None], ss[None], cs[None])
```

```python
import functools

import numpy as np
import jax
import jax.numpy as jnp
from jax import lax
from jax.experimental import pallas as pl
from jax.experimental.pallas import tpu as pltpu

F32 = jnp.float32
BF16 = jnp.bfloat16

D_MODEL = 1024
HEADS = 4
DK = 128
DV = 256
KEY_DIM = HEADS * DK
VAL_DIM = HEADS * DV
RANK = 16
TAU = 16.0
CHUNK = 64
SUB = 16
CONV_W = 31
HIST = CONV_W - 1
PLE = 256
EPS = 1e-6
LOG2E = 1.4426950408889634
LANES = 128
SUBLANES = 8
N_LB = D_MODEL // LANES
MXU_SLICE = 256
VMEM_LIMIT = 62 * 1024 * 1024


def _tiles():
    return dict(tt=256, tm=256, nb=8, nbs=16)


def _dot(a, b):
    return jnp.dot(a, b, preferred_element_type=F32)


def _sigmoid(x):
    return 1.0 / (1.0 + jnp.exp(-x))


def _silu(x):
    return x * _sigmoid(x)


def _log_sigmoid(x):
    return -(jnp.maximum(-x, 0.0) + jnp.log1p(jnp.exp(-jnp.abs(x))))


def _rms(x, g):
    return x * lax.rsqrt(jnp.mean(x * x, axis=-1, keepdims=True) + EPS) * g


def _qkv_la(h, wa_ref, wr_ref, wup_ref, bup_ref):
    q = _dot(h, wa_ref[:, 0:KEY_DIM]) * (DK ** -0.5)
    k = _dot(h, wa_ref[:, KEY_DIM:2 * KEY_DIM])
    v = _dot(h, wa_ref[:, 2 * KEY_DIM:2 * KEY_DIM + VAL_DIM]).astype(BF16)
    r = _dot(h, wr_ref[...]).astype(BF16)
    la = _log_sigmoid(_dot(r, wup_ref[...]) + bup_ref[...]) * (LOG2E / TAU)
    return q, k, la, v


def _z_a(h, wa_ref):
    return _silu(_dot(h, wa_ref[:, 2 * KEY_DIM + VAL_DIM:]))


def _glu(h, wb_ref):
    c = D_MODEL
    return _dot(h, wb_ref[:, 0:c]) * _sigmoid(_dot(h, wb_ref[:, c:2 * c]))


def _gated_heads(o, sza, ggla_ref):
    g = ggla_ref[...]
    on = jnp.concatenate([_rms(o[:, hd * DV:(hd + 1) * DV], g) for hd in range(HEADS)], axis=1)
    return (on * sza).astype(BF16)


def _conv_branch_in(cv, szb, gln_ref, bln_ref):
    mu = jnp.mean(cv, axis=-1, keepdims=True)
    xc = cv - mu
    ln = xc * lax.rsqrt(jnp.mean(xc * xc, axis=-1, keepdims=True) + EPS) * gln_ref[...] + bln_ref[...]
    return (_silu(ln) * szb).astype(BF16)


def _tail(x, y_a, cv, szb, sga, sgb, pe, gln_ref, bln_ref, wbo_ref, wo_ref, wpg_ref, gfin_ref):
    y_b = _dot(_conv_branch_in(cv, szb, gln_ref, bln_ref), wbo_ref[...])
    merged = sga * y_a + sgb * y_b
    x1 = x + _dot(merged.astype(BF16), wo_ref[...])
    x2 = x1 + _sigmoid(_dot(x1.astype(BF16), wpg_ref[...])) * pe
    return _rms(x2, gfin_ref[...])


def _gla_chunk(q, k, a, v, s_list, tri, emat, mcat_ref, n_keys, tick=lambda: None):
    c = q.shape[0]
    n_sub = c // SUB
    half = SUB // 2
    a0 = a.astype(BF16)
    r1 = a - a0.astype(F32)
    a1 = r1.astype(BF16)
    a2 = (r1 - a1.astype(F32)).astype(BF16)
    b = _dot(tri, a0) + _dot(tri, a1) + _dot(tri, a2)

    att_off, q_dec, k_dec, d_last = [], [], [], []
    for h in range(HEADS):
        sl = slice(h * DK, (h + 1) * DK)
        qh, kh, bh = q[:, sl], k[:, sl], b[:, sl]
        bl = bh[c - 1:c, :]
        q_dec.append((qh * jnp.exp2(bh)).astype(BF16))
        k_dec.append(kh * jnp.exp2(bl - bh))
        d_last.append(jnp.exp2(bl))
        rows = [jnp.zeros((SUB, c), F32)]
        for i in range(1, n_sub):
            lo = i * SUB
            br = bh[lo - 1:lo, :]
            qi = (qh[lo:lo + SUB] * jnp.exp2(bh[lo:lo + SUB] - br)).astype(BF16)
            ki = (kh[:lo] * jnp.exp2(br - bh[:lo])).astype(BF16)
            kfull = jnp.concatenate([ki, jnp.zeros((c - lo, DK), BF16)], axis=0)
            rows.append(lax.dot_general(qi, kfull, (((1,), (1,)), ((), ())),
                                        preferred_element_type=F32))
        att_off.append(jnp.concatenate(rows, axis=0) if n_sub > 1 else rows[0])
        for i in range(n_sub):
            lo = i * SUB
            for s in range(n_keys):
                top = 0 if s < half else half
                qb, bb = qh[lo + top:lo + SUB], bh[lo + top:lo + SUB]
                dec = jnp.exp2(jnp.minimum(bb - bh[lo + s:lo + s + 1, :], 0.0))
                m = qb * dec * kh[lo + s:lo + s + 1, :]
                if top:
                    m = jnp.concatenate([jnp.zeros((top, DK), F32), m], axis=0)
                mcat_ref[h * c + lo:h * c + lo + SUB, s * DK:(s + 1) * DK] = m.astype(BF16)
        tick()

    rsum = _dot(mcat_ref[:, 0:n_keys * DK], emat[0:n_keys * DK, :])
    ti = lax.broadcasted_iota(jnp.int32, (c, c), 0)
    si = lax.broadcasted_iota(jnp.int32, (c, c), 1)
    diag_mask = (ti // SUB == si // SUB) & (si <= ti)

    o_parts, s_new = [], []
    for h in range(HEADS):
        vh = v[:, h * DV:(h + 1) * DV]
        att = att_off[h] + jnp.where(diag_mask, rsum[h * c:(h + 1) * c, 0:c], 0.0)
        lhs = jnp.concatenate([q_dec[h], att.astype(BF16)], axis=1)
        rhs = jnp.concatenate([s_list[h].astype(BF16), vh], axis=0)
        o_parts.append(_dot(lhs, rhs))
        dcol = jnp.broadcast_to(d_last[h], (DK, DK)).T
        kt = k_dec[h].T.astype(BF16)
        s_new.append(jnp.concatenate([dcol, dcol], axis=1) * s_list[h] + _dot(kt, vh))
    return jnp.concatenate(o_parts, axis=1), s_new


def _tri_const(c):
    return jnp.asarray(np.tril(np.ones((c, c), np.float32)), BF16)


def _emat_const():
    e = np.zeros((SUB * DK, LANES), np.float32)
    rows = np.arange(SUB * DK)
    for l in range(LANES):
        e[rows[rows // DK == l % SUB], l] = 1.0
    return jnp.asarray(e, BF16)


def _conv_tile(u_ref, xs_ref, hist_ref, cvs_ref, wdwb_ref, bdwb_ref, tt, tick=lambda: None):
    seg = tt // SUBLANES
    x0 = HIST * SUBLANES
    sub7 = lax.broadcasted_iota(jnp.int32, (SUBLANES, LANES), 0) == SUBLANES - 1
    grp = SUBLANES
    for cb in range(N_LB):
        cs = slice(cb * LANES, (cb + 1) * LANES)
        for s in range(SUBLANES):
            for a0 in range(0, seg, SUBLANES):
                val = u_ref[seg * s + a0:seg * s + a0 + SUBLANES, cs]
                xs_ref[cb, pl.ds(x0 + SUBLANES * a0 + s, SUBLANES, stride=SUBLANES), :] = val
        for a in range(seg - HIST, seg):
            e = a - (seg - HIST)
            new = xs_ref[cb, x0 + SUBLANES * a:x0 + SUBLANES * (a + 1), :]
            old = hist_ref[cb, SUBLANES * e:SUBLANES * (e + 1), :]
            xs_ref[cb, SUBLANES * e:SUBLANES * (e + 1), :] = pltpu.roll(jnp.where(sub7, old, new), 1, axis=0)
            hist_ref[cb, SUBLANES * e:SUBLANES * (e + 1), :] = new
        w = [wdwb_ref[SUBLANES * j:SUBLANES * (j + 1), cs] for j in range(CONV_W)]
        for g0 in range(0, seg, grp):
            if g0 % (seg // 2) == 0:
                tick()
            acc = [bdwb_ref[:, cs] for _ in range(grp)]
            for d in range(g0, g0 + grp + HIST):
                x = xs_ref[cb, SUBLANES * d:SUBLANES * (d + 1), :]
                for i in range(grp):
                    j = d - (g0 + i)
                    if 0 <= j < CONV_W:
                        acc[i] = acc[i] + w[j] * x
            for i in range(grp):
                cvs_ref[cb, pl.ds(g0 + i, SUBLANES, stride=seg), :] = acc[i]


def _split_win(win_ref):
    n_a = 2 * KEY_DIM + 2 * VAL_DIM
    n_b = 5 * D_MODEL
    return (win_ref.at[:, 0:n_a], win_ref.at[:, n_a:n_a + n_b], win_ref.at[:, n_a + n_b:n_a + n_b + LANES])


def _split_wsq(wsq_ref):
    return wsq_ref.at[0], wsq_ref.at[1], wsq_ref.at[2], wsq_ref.at[3]


def _prompt_kernel(x_ref, xb_ref, pb_ref, gpre_ref, win_ref, wup_ref, bup_ref, tri_ref, emat_ref,
                   wdwb_ref, bdwb_ref, ggla_ref, gln_ref, bln_ref, wsq_ref, wpe_ref,
                   gfin_ref,
                   y_ref, s_ref, cs_ref,
                   q_ref, k_ref, la_ref, v_ref, o_ref, mcat_ref, xs_ref, hist_ref, cvs_ref, sst_ref,
                   st_yb_ref, st_ma_ref, st_sgb_ref, u_ref, sza_ref, szb_ref, mg_ref, x1_ref, x1b_ref,
                   *, tt, n_t, n_tiles):
    wa_ref, wb_ref, wr_ref = _split_win(win_ref)
    wao_ref, wbo_ref, wo_ref, wpg_ref = _split_wsq(wsq_ref)
    i = pl.program_id(0)
    live = i < n_tiles
    t = jnp.minimum(i, n_tiles - 1) % n_t

    @pl.when(jnp.logical_and(t == 0, live))
    def _():
        sst_ref[...] = jnp.zeros(sst_ref.shape, F32)
        hist_ref[...] = jnp.zeros(hist_ref.shape, F32)

    @pl.when(i == 0)
    def _():
        st_yb_ref[...] = jnp.zeros(st_yb_ref.shape, BF16)
        st_ma_ref[...] = jnp.zeros(st_ma_ref.shape, F32)
        st_sgb_ref[...] = jnp.zeros(st_sgb_ref.shape, F32)

    x = x_ref[0]
    h = _rms(x, gpre_ref[...]).astype(BF16)
    q, k, la, v = _qkv_la(h, wa_ref, wr_ref, wup_ref, bup_ref)
    q_ref[...] = q
    k_ref[...] = k
    la_ref[...] = la
    v_ref[...] = v

    c = D_MODEL
    za0 = 2 * KEY_DIM + VAL_DIM

    def sliced(store, fn):
        return [functools.partial(lambda lo: store(lo, lo + MXU_SLICE, fn(lo, lo + MXU_SLICE)), n * MXU_SLICE)
                for n in range(c // MXU_SLICE)]

    def into(ref, *lead):
        def store(lo, hi, val):
            ref[(*lead, slice(None), slice(lo, hi))] = val
        return store

    def ticker(jobs, n_ticks):
        state = [0, 0]

        def tick():
            state[0] += 1
            upto = -(-len(jobs) * state[0] // n_ticks)
            while state[1] < min(upto, len(jobs)):
                jobs[state[1]]()
                state[1] += 1
        return tick

    pb = pb_ref[0].astype(BF16)
    back1 = sliced(into(mg_ref), lambda lo, hi: (
        st_ma_ref[:, lo:hi] + st_sgb_ref[:, lo:hi] * _dot(st_yb_ref[...], wbo_ref[:, lo:hi])).astype(BF16))

    def store_x1(lo, hi, val):
        x1_ref[:, lo:hi] = val
        x1b_ref[:, lo:hi] = val.astype(BF16)

    back2 = sliced(store_x1, lambda lo, hi: xb_ref[0, :, lo:hi] + _dot(mg_ref[...], wo_ref[:, lo:hi]))
    back3 = sliced(into(y_ref, 0), lambda lo, hi: (
        x1_ref[:, lo:hi] + _sigmoid(_dot(x1b_ref[...], wpg_ref[:, lo:hi])) * _dot(pb, wpe_ref[:, lo:hi])))

    n_chunks = tt // CHUNK
    jobs = (sliced(into(u_ref), lambda lo, hi: (_dot(h, wb_ref[:, lo:hi])
                                                * _sigmoid(_dot(h, wb_ref[:, c + lo:c + hi]))))
            + back1 + sliced(into(sza_ref), lambda lo, hi: _silu(_dot(h, wa_ref[:, za0 + lo:za0 + hi]))) + back2)
    tick = ticker(jobs, n_chunks * HEADS)
    for ci in range(n_chunks):
        rows = slice(ci * CHUNK, (ci + 1) * CHUNK)
        s_list = [sst_ref[hd] for hd in range(HEADS)]
        o, s_new = _gla_chunk(q_ref[rows, :], k_ref[rows, :], la_ref[rows, :], v_ref[rows, :],
                              s_list, tri_ref[...], emat_ref, mcat_ref, SUB, tick)
        o_ref[rows, :] = o
        for hd in range(HEADS):
            sst_ref[hd] = s_new[hd]

    ona = _gated_heads(o_ref[...], sza_ref[...], ggla_ref)
    jobs = (sliced(into(st_ma_ref), lambda lo, hi: _sigmoid(_dot(h, wb_ref[:, 3 * c + lo:3 * c + hi])))
            + back3
            + sliced(into(st_ma_ref), lambda lo, hi: st_ma_ref[:, lo:hi] * _dot(ona, wao_ref[:, lo:hi]))
            + sliced(into(szb_ref), lambda lo, hi: _silu(_dot(h, wb_ref[:, 2 * c + lo:2 * c + hi])))
            + sliced(into(st_sgb_ref), lambda lo, hi: _sigmoid(_dot(h, wb_ref[:, 4 * c + lo:4 * c + hi]))))
    tick = ticker(jobs, 2 * N_LB)
    _conv_tile(u_ref, xs_ref, hist_ref, cvs_ref, wdwb_ref, bdwb_ref, tt, tick)
    y_ref[0] = _rms(y_ref[0], gfin_ref[...])

    cv = jnp.concatenate([cvs_ref[cb] for cb in range(N_LB)], axis=1)
    st_yb_ref[...] = _conv_branch_in(cv, szb_ref[...], gln_ref, bln_ref)

    @pl.when(live)
    def _():
        s_ref[0] = sst_ref[...]
        cs_ref[0] = u_ref[tt - HIST:tt, :]


def _whole():
    return pl.BlockSpec(memory_space=pltpu.VMEM)


def _prompt_call(x, p, w, tt):
    bsz, t_len, _ = x.shape
    seg = tt // SUBLANES
    assert t_len % tt == 0 and tt % CHUNK == 0 and seg >= HIST and seg % SUBLANES == 0
    n_t = t_len // tt
    n_tiles = bsz * n_t
    front = lambda i: jnp.minimum(i, n_tiles - 1)
    back = lambda i: jnp.maximum(i - 1, 0)
    tok = lambda wd, tile: pl.BlockSpec((1, tt, wd), lambda i: (tile(i) // n_t, tile(i) % n_t, 0))
    names = ["gpre", "win", "wup", "bup", "tri", "emat", "wdwb", "bdwb", "ggla", "gln", "bln", "wsq", "wpe",
             "gfin"]
    return pl.pallas_call(
        functools.partial(_prompt_kernel, tt=tt, n_t=n_t, n_tiles=n_tiles),
        grid=(n_tiles + 1,),
        in_specs=[tok(D_MODEL, front), tok(D_MODEL, back), tok(PLE, back)] + [_whole()] * len(names),
        out_specs=[tok(D_MODEL, back),
                   pl.BlockSpec((1, HEADS, DK, DV), lambda i: (front(i) // n_t, 0, 0, 0)),
                   pl.BlockSpec((1, HIST, D_MODEL), lambda i: (front(i) // n_t, 0, 0))],
        out_shape=[jax.ShapeDtypeStruct((bsz, t_len, D_MODEL), F32),
                   jax.ShapeDtypeStruct((bsz, HEADS, DK, DV), F32),
                   jax.ShapeDtypeStruct((bsz, HIST, D_MODEL), F32)],
        scratch_shapes=[pltpu.VMEM((tt, KEY_DIM), F32), pltpu.VMEM((tt, KEY_DIM), F32),
                        pltpu.VMEM((tt, KEY_DIM), F32), pltpu.VMEM((tt, VAL_DIM), BF16),
                        pltpu.VMEM((tt, VAL_DIM), F32),
                        pltpu.VMEM((HEADS * CHUNK, SUB * DK), BF16),
                        pltpu.VMEM((N_LB, (HIST + seg) * SUBLANES, LANES), F32),
                        pltpu.VMEM((N_LB, HIST * SUBLANES, LANES), F32),
                        pltpu.VMEM((N_LB, tt, LANES), F32),
                        pltpu.VMEM((HEADS, DK, DV), F32),
                        pltpu.VMEM((tt, D_MODEL), BF16),
                        pltpu.VMEM((tt, D_MODEL), F32),
                        pltpu.VMEM((tt, D_MODEL), F32),
                        pltpu.VMEM((tt, D_MODEL), F32), pltpu.VMEM((tt, D_MODEL), F32),
                        pltpu.VMEM((tt, D_MODEL), F32), pltpu.VMEM((tt, D_MODEL), BF16),
                        pltpu.VMEM((tt, D_MODEL), F32), pltpu.VMEM((tt, D_MODEL), BF16)],
        compiler_params=pltpu.CompilerParams(dimension_semantics=("arbitrary",),
                                             vmem_limit_bytes=VMEM_LIMIT),
        name="prompt_layer",
    )(x, x, p, *[w[n] for n in names])


def _pre_kernel(x_ref, gpre_ref, win_ref, wup_ref, bup_ref,
                q_ref, k_ref, la_ref, v_ref, u_ref):
    wa_ref, wb_ref, wr_ref = _split_win(win_ref)
    h = _rms(x_ref[...], gpre_ref[...]).astype(BF16)
    q_ref[...], k_ref[...], la_ref[...], v_ref[...] = _qkv_la(h, wa_ref, wr_ref, wup_ref, bup_ref)
    u_ref[...] = _glu(h, wb_ref)


def _pre_call(x2d, w, tm):
    n = x2d.shape[0]
    row = lambda wd: pl.BlockSpec((tm, wd), lambda i: (i, 0))
    outs = [(KEY_DIM, F32), (KEY_DIM, F32), (KEY_DIM, F32), (VAL_DIM, BF16), (D_MODEL, F32)]
    names = ["gpre", "win", "wup", "bup"]
    return pl.pallas_call(
        _pre_kernel,
        grid=(n // tm,),
        in_specs=[row(D_MODEL)] + [_whole()] * len(names),
        out_specs=[row(wd) for wd, _ in outs],
        out_shape=[jax.ShapeDtypeStruct((n, wd), dt) for wd, dt in outs],
        compiler_params=pltpu.CompilerParams(dimension_semantics=("arbitrary",),
                                             vmem_limit_bytes=VMEM_LIMIT),
        name="sample_pre",
    )(x2d, *[w[n_] for n_ in names])


def _sample_conv_kernel(c0_ref, u_ref, wdwb_ref, bdwb_ref, cv_ref, cs_ref, *, t_len, nbs):
    ext = lambda d, rows, cs: c0_ref[d, rows, cs] if d < HIST else u_ref[d - HIST, rows, cs]
    for r0 in range(0, nbs, SUBLANES):
        rows = slice(r0, r0 + SUBLANES)
        for cb in range(N_LB):
            cs = slice(cb * LANES, (cb + 1) * LANES)
            acc = [bdwb_ref[:, cs] for _ in range(t_len)]
            for d in range(HIST + t_len):
                x = ext(d, rows, cs)
                for t in range(t_len):
                    j = d - t
                    if 0 <= j < CONV_W:
                        acc[t] = acc[t] + wdwb_ref[SUBLANES * j:SUBLANES * (j + 1), cs] * x
            for t in range(t_len):
                cv_ref[t, rows, cs] = acc[t]
    for d in range(HIST):
        cs_ref[d] = ext(d + t_len, slice(None), slice(None))


def _sample_conv_call(c0t, ut, w, nbs):
    _, bsz, _ = c0t.shape
    t_len = ut.shape[0]
    assert bsz % nbs == 0 and nbs % SUBLANES == 0
    blk = lambda rows: pl.BlockSpec((rows, nbs, D_MODEL), lambda i: (0, i, 0))
    return pl.pallas_call(
        functools.partial(_sample_conv_kernel, t_len=t_len, nbs=nbs),
        grid=(bsz // nbs,),
        in_specs=[blk(HIST), blk(t_len), _whole(), _whole()],
        out_specs=[blk(t_len), blk(HIST)],
        out_shape=[jax.ShapeDtypeStruct((t_len, bsz, D_MODEL), F32),
                   jax.ShapeDtypeStruct((HIST, bsz, D_MODEL), F32)],
        compiler_params=pltpu.CompilerParams(dimension_semantics=("arbitrary",),
                                             vmem_limit_bytes=VMEM_LIMIT),
        name="sample_conv",
    )(c0t, ut, w["wdwb"], w["bdwb"])


def _mix_sample_kernel(q_ref, k_ref, la_ref, v_ref, s0_ref, tri_ref, emat_ref,
                       o_ref, s_ref, mcat_ref, *, nb, t_len):
    pad = SUB - t_len

    def padf(z):
        zf = z.astype(F32)
        return jnp.concatenate([zf, jnp.zeros((pad, z.shape[1]), F32)], axis=0).astype(z.dtype)

    def pair(m, carry):
        for slot in range(2):
            n = 2 * m + slot
            s_list = [s0_ref[n, hd] for hd in range(HEADS)]
            o, s_new = _gla_chunk(padf(q_ref[n]), padf(k_ref[n]), padf(la_ref[n]), padf(v_ref[n]),
                                  s_list, tri_ref[...], emat_ref, mcat_ref.at[slot], t_len)
            o_ref[n] = o[0:t_len]
            for hd in range(HEADS):
                s_ref[n, hd] = s_new[hd]
        return carry

    lax.fori_loop(0, nb // 2, pair, 0)


def _mix_sample_call(q, k, la, v, s0, w, nb):
    bsz, t_len, _ = q.shape
    assert bsz % nb == 0 and nb % 2 == 0 and t_len <= SUB // 2
    tok = lambda wd: pl.BlockSpec((nb, t_len, wd), lambda i: (i, 0, 0))
    st = pl.BlockSpec((nb, HEADS, DK, DV), lambda i: (i, 0, 0, 0))
    return pl.pallas_call(
        functools.partial(_mix_sample_kernel, nb=nb, t_len=t_len),
        grid=(bsz // nb,),
        in_specs=[tok(KEY_DIM), tok(KEY_DIM), tok(KEY_DIM), tok(VAL_DIM), st, _whole(), _whole()],
        out_specs=[tok(VAL_DIM), st],
        out_shape=[jax.ShapeDtypeStruct((bsz, t_len, VAL_DIM), F32),
                   jax.ShapeDtypeStruct((bsz, HEADS, DK, DV), F32)],
        scratch_shapes=[pltpu.VMEM((2, HEADS * SUB, SUB * DK), BF16)],
        compiler_params=pltpu.CompilerParams(dimension_semantics=("arbitrary",),
                                             vmem_limit_bytes=VMEM_LIMIT),
        name="sample_mix",
    )(q, k, la, v, s0, _tri_const(SUB), w["emat"])


def _post_kernel(x_ref, o_ref, cv_ref, p_ref, gpre_ref, win_ref,
                 ggla_ref, gln_ref, bln_ref, wsq_ref, wpe_ref, gfin_ref, y_ref):
    wa_ref, wb_ref, _ = _split_win(win_ref)
    wao_ref, wbo_ref, wo_ref, wpg_ref = _split_wsq(wsq_ref)
    x = x_ref[...]
    c = D_MODEL
    h = _rms(x, gpre_ref[...]).astype(BF16)
    y_a = _dot(_gated_heads(o_ref[...], _z_a(h, wa_ref), ggla_ref), wao_ref[...])
    szb = _silu(_dot(h, wb_ref[:, 2 * c:3 * c]))
    sga = _sigmoid(_dot(h, wb_ref[:, 3 * c:4 * c]))
    sgb = _sigmoid(_dot(h, wb_ref[:, 4 * c:5 * c]))
    pe = _dot(p_ref[...].astype(BF16), wpe_ref[...])
    y_ref[...] = _tail(x, y_a, cv_ref[...], szb, sga, sgb, pe, gln_ref, bln_ref, wbo_ref, wo_ref, wpg_ref, gfin_ref)


def _post_call(x2d, o, cv, p2d, w, tm):
    n = x2d.shape[0]
    row = lambda wd: pl.BlockSpec((tm, wd), lambda i: (i, 0))
    names = ["gpre", "win", "ggla", "gln", "bln", "wsq", "wpe", "gfin"]
    return pl.pallas_call(
        _post_kernel,
        grid=(n // tm,),
        in_specs=[row(D_MODEL)] * 3 + [row(PLE)] + [_whole()] * len(names),
        out_specs=row(D_MODEL),
        out_shape=jax.ShapeDtypeStruct((n, D_MODEL), F32),
        compiler_params=pltpu.CompilerParams(dimension_semantics=("arbitrary",),
                                             vmem_limit_bytes=VMEM_LIMIT),
        name="sample_post",
    )(x2d, o, cv, p2d, *[w[n_] for n_ in names])


def _sample_layer(x, p, s0, c0, w, tm, nb, nbs):
    bsz, t_len, _ = x.shape
    x2d = x.reshape(bsz * t_len, D_MODEL)
    q, k, la, v, u = _pre_call(x2d, w, tm)
    r3 = lambda z: z.reshape(bsz, t_len, z.shape[-1])
    tmaj = lambda z: jnp.transpose(z, (1, 0, 2))
    o, s_new = _mix_sample_call(r3(q), r3(k), r3(la), r3(v), s0, w, nb)
    cv_t, c_new_t = _sample_conv_call(tmaj(c0), tmaj(r3(u)), w, nbs)
    y = _post_call(x2d, o.reshape(-1, VAL_DIM), tmaj(cv_t).reshape(-1, D_MODEL), p.reshape(bsz * t_len, PLE), w, tm)
    return y.reshape(bsz, t_len, D_MODEL), s_new, tmaj(c_new_t)


def kernel(x_prompt, x_sample, state_gla, state_conv, p_prompt, p_sample, g_pre, w_in, w_a_up, b_a_up, g_gla, w_a_out, w_dw, b_dw, g_ln, b_ln, w_b_out, w_o, w_pe, w_pg, g_final):
    depth = w_in.shape[0]
    assert depth == 1, "one trunk layer"
    i = 0
    tl = _tiles()
    n_a = 2 * KEY_DIM + 2 * VAL_DIM
    row = lambda z: z.reshape(1, -1)
    w = {
        "gpre": row(g_pre[i]),
        "win": jnp.concatenate([w_in[i][:, :n_a], w_in[i][:, n_a + RANK:], w_in[i][:, n_a:n_a + RANK],
                                jnp.zeros((D_MODEL, LANES - RANK), w_in.dtype)], axis=1).astype(BF16),
        "wup": jnp.pad(w_a_up[i], ((0, LANES - RANK), (0, 0))).astype(BF16),
        "bup": row(b_a_up[i]),
        "tri": _tri_const(CHUNK), "emat": _emat_const(),
        "wdwb": jnp.repeat(w_dw[i], SUBLANES, axis=0),
        "bdwb": jnp.broadcast_to(row(b_dw[i]), (SUBLANES, D_MODEL)),
        "ggla": row(g_gla[i]), "gln": row(g_ln[i]), "bln": row(b_ln[i]),
        "wsq": jnp.stack([w_a_out[i], w_b_out[i], w_o[i], w_pg[i]]).astype(BF16),
        "wpe": w_pe[i].astype(BF16), "gfin": row(g_final),
    }
    yp, sp, cp = _prompt_call(x_prompt, p_prompt[i], w, tl["tt"])
    ys, ss, cs = _sample_layer(x_sample, p_sample[i], state_gla[i], state_conv[i], w, tl["tm"], tl["nb"],
                               tl["nbs"])
    return (yp, ys, sp[None], cp[None], ss[None], cs[None])
```

```python
import functools

import numpy as np
import jax
import jax.numpy as jnp
from jax import lax
from jax.experimental import pallas as pl
from jax.experimental.pallas import tpu as pltpu

F32 = jnp.float32
BF16 = jnp.bfloat16

D_MODEL = 1024
HEADS = 4
DK = 128
DV = 256
KEY_DIM = HEADS * DK
VAL_DIM = HEADS * DV
RANK = 16
TAU = 16.0
CHUNK = 64
SUB = 8
PACK_ROWS = 16
CONV_W = 31
HIST = CONV_W - 1
PLE = 256
EPS = 1e-6
LOG2E = 1.4426950408889634
LANES = 128
SUBLANES = 8
N_LB = D_MODEL // LANES
MXU_SLICE = 256
VMEM_LIMIT = 62 * 1024 * 1024


def _tiles():
    return dict(tt=256, tm=256, nb=8, nbs=16)


def _dot(a, b):
    return jnp.dot(a, b, preferred_element_type=F32)


def _sigmoid(x):
    return 1.0 / (1.0 + jnp.exp2(x * (-LOG2E)))


def _silu(x):
    return x * _sigmoid(x)


def _log_sigmoid(x):
    return -(jnp.maximum(-x, 0.0) + jnp.log1p(jnp.exp(-jnp.abs(x))))


def _rms(x, g):
    return x * lax.rsqrt(jnp.mean(x * x, axis=-1, keepdims=True) + EPS) * g


def _qkv_la(h, wa_ref, wr_ref, wup_ref, bup_ref):
    q = _dot(h, wa_ref[:, 0:KEY_DIM]) * (DK ** -0.5)
    k = _dot(h, wa_ref[:, KEY_DIM:2 * KEY_DIM])
    v = _dot(h, wa_ref[:, 2 * KEY_DIM:2 * KEY_DIM + VAL_DIM]).astype(BF16)
    r = _dot(h, wr_ref[...]).astype(BF16)
    la = _log_sigmoid(_dot(r, wup_ref[...]) + bup_ref[...]) * (LOG2E / TAU)
    return q, k, la, v


def _z_a(h, wa_ref):
    return _silu(_dot(h, wa_ref[:, 2 * KEY_DIM + VAL_DIM:]))


def _glu(h, wb_ref):
    c = D_MODEL
    return _dot(h, wb_ref[:, 0:c]) * _sigmoid(_dot(h, wb_ref[:, c:2 * c]))


def _gated_heads(o, sza, ggla_ref):
    g = ggla_ref[...]
    on = jnp.concatenate([_rms(o[:, hd * DV:(hd + 1) * DV], g) for hd in range(HEADS)], axis=1)
    return (on * sza).astype(BF16)


def _conv_branch_in(cv, szb, gln_ref, bln_ref):
    mu = jnp.mean(cv, axis=-1, keepdims=True)
    xc = cv - mu
    ln = xc * lax.rsqrt(jnp.mean(xc * xc, axis=-1, keepdims=True) + EPS) * gln_ref[...] + bln_ref[...]
    return (_silu(ln) * szb).astype(BF16)


def _tail(x, y_a, cv, szb, sga, sgb, pe, gln_ref, bln_ref, wbo_ref, wo_ref, wpg_ref, gfin_ref):
    y_b = _dot(_conv_branch_in(cv, szb, gln_ref, bln_ref), wbo_ref[...])
    merged = sga * y_a + sgb * y_b
    x1 = x + _dot(merged.astype(BF16), wo_ref[...])
    x2 = x1 + _sigmoid(_dot(x1.astype(BF16), wpg_ref[...])) * pe
    return _rms(x2, gfin_ref[...])


def _gla_chunk(q, k, a, v, s_list, tri, emat, mcat_ref, n_keys, tick=lambda: None):
    c = q.shape[0]
    n_sub = c // SUB
    k_used = min(SUB, n_keys)
    a0 = a.astype(BF16)
    r1 = a - a0.astype(F32)
    a1 = r1.astype(BF16)
    a2 = (r1 - a1.astype(F32)).astype(BF16)
    b = _dot(tri, a0) + _dot(tri, a1) + _dot(tri, a2)

    att_off, q_dec, k_dec, d_last = [], [], [], []
    for h in range(HEADS):
        sl = slice(h * DK, (h + 1) * DK)
        qh, kh, bh = q[:, sl], k[:, sl], b[:, sl]
        bl = bh[c - 1:c, :]
        q_dec.append((qh * jnp.exp2(bh)).astype(BF16))
        k_dec.append(kh * jnp.exp2(bl - bh))
        d_last.append(jnp.exp2(bl))
        zrow = jnp.zeros((SUB, DK), F32)
        rows = [jnp.zeros((SUB, c), F32)]
        for i in range(1, n_sub):
            lo = i * SUB
            if lo >= n_keys:
                rows.append(jnp.zeros((SUB, c), F32))
                continue
            br = bh[lo - 1:lo, :]
            qi = jnp.concatenate([qh[lo:lo + SUB] * jnp.exp2(bh[lo:lo + SUB] - br), zrow], axis=0).astype(BF16)
            ki = kh[:lo] * jnp.exp2(br - bh[:lo])
            kfull = jnp.concatenate([ki, jnp.zeros((c - lo, DK), F32)], axis=0).astype(BF16)
            rows.append(lax.dot_general(qi, kfull, (((1,), (1,)), ((), ())),
                                        preferred_element_type=F32)[0:SUB])
        att_off.append(jnp.concatenate(rows, axis=0))
        for g0 in range(0, c, PACK_ROWS):
            for s in range(k_used):
                halves = []
                for lo in (g0, g0 + SUB):
                    if lo + s < n_keys:
                        qb, bb = qh[lo:lo + SUB], bh[lo:lo + SUB]
                        dec = jnp.exp2(jnp.minimum(bb - bh[lo + s:lo + s + 1, :], 0.0))
                        halves.append(qb * dec * kh[lo + s:lo + s + 1, :])
                    else:
                        halves.append(zrow)
                mcat_ref[h * c + g0:h * c + g0 + PACK_ROWS, s * DK:(s + 1) * DK] = (
                    jnp.concatenate(halves, axis=0).astype(BF16))
        tick()

    rsum = _dot(mcat_ref[:, 0:k_used * DK], emat[0:k_used * DK, :])
    ti = lax.broadcasted_iota(jnp.int32, (c, c), 0)
    si = lax.broadcasted_iota(jnp.int32, (c, c), 1)
    diag_mask = (ti // SUB == si // SUB) & (si <= ti)

    o_parts, s_new = [], []
    for h in range(HEADS):
        vh = v[:, h * DV:(h + 1) * DV]
        att = att_off[h] + jnp.where(diag_mask, rsum[h * c:(h + 1) * c, 0:c], 0.0)
        lhs = jnp.concatenate([q_dec[h], att.astype(BF16)], axis=1)
        rhs = jnp.concatenate([s_list[h].astype(BF16), vh], axis=0)
        o_parts.append(_dot(lhs, rhs))
        dcol = jnp.broadcast_to(d_last[h], (DK, DK)).T
        kt = k_dec[h].T.astype(BF16)
        s_new.append(jnp.concatenate([dcol, dcol], axis=1) * s_list[h] + _dot(kt, vh))
    return jnp.concatenate(o_parts, axis=1), s_new


def _tri_const(c):
    return jnp.asarray(np.tril(np.ones((c, c), np.float32)), BF16)


def _emat_const():
    e = np.zeros((SUB * DK, LANES), np.float32)
    rows = np.arange(SUB * DK)
    for l in range(LANES):
        e[rows[rows // DK == l % SUB], l] = 1.0
    return jnp.asarray(e, BF16)


def _conv_tile(u_ref, xs_ref, hist_ref, cvs_ref, wdwb_ref, bdwb_ref, tt, tick=lambda: None):
    seg = tt // SUBLANES
    x0 = HIST * SUBLANES
    sub7 = lax.broadcasted_iota(jnp.int32, (SUBLANES, LANES), 0) == SUBLANES - 1
    grp = SUBLANES
    for cb in range(N_LB):
        cs = slice(cb * LANES, (cb + 1) * LANES)
        for s in range(SUBLANES):
            for a0 in range(0, seg, SUBLANES):
                val = u_ref[seg * s + a0:seg * s + a0 + SUBLANES, cs]
                xs_ref[cb, pl.ds(x0 + SUBLANES * a0 + s, SUBLANES, stride=SUBLANES), :] = val
        for a in range(seg - HIST, seg):
            e = a - (seg - HIST)
            new = xs_ref[cb, x0 + SUBLANES * a:x0 + SUBLANES * (a + 1), :]
            old = hist_ref[cb, SUBLANES * e:SUBLANES * (e + 1), :]
            xs_ref[cb, SUBLANES * e:SUBLANES * (e + 1), :] = pltpu.roll(jnp.where(sub7, old, new), 1, axis=0)
            hist_ref[cb, SUBLANES * e:SUBLANES * (e + 1), :] = new
        w = [wdwb_ref[SUBLANES * j:SUBLANES * (j + 1), cs] for j in range(CONV_W)]
        for g0 in range(0, seg, grp):
            if g0 % (seg // 2) == 0:
                tick()
            acc = [bdwb_ref[:, cs] for _ in range(grp)]
            for d in range(g0, g0 + grp + HIST):
                x = xs_ref[cb, SUBLANES * d:SUBLANES * (d + 1), :]
                for i in range(grp):
                    j = d - (g0 + i)
                    if 0 <= j < CONV_W:
                        acc[i] = acc[i] + w[j] * x
            for i in range(grp):
                cvs_ref[cb, pl.ds(g0 + i, SUBLANES, stride=seg), :] = acc[i]


def _split_wsq(wsq_ref):
    return wsq_ref.at[0], wsq_ref.at[1], wsq_ref.at[2], wsq_ref.at[3]


def _prompt_kernel(x_ref, xb_ref, pb_ref, gpre_ref, wa_ref, wb_ref, wr_ref, wup_ref, bup_ref, tri_ref, emat_ref,
                   wdwb_ref, bdwb_ref, ggla_ref, gln_ref, bln_ref, wsq_ref, wpe_ref,
                   gfin_ref,
                   y_ref, s_ref, cs_ref,
                   q_ref, k_ref, la_ref, v_ref, o_ref, mcat_ref, xs_ref, hist_ref, cvs_ref, sst_ref,
                   st_yb_ref, st_ma_ref, st_sgb_ref, u_ref, sza_ref, szb_ref, mg_ref, x1_ref, x1b_ref,
                   *, tt, n_t, n_tiles):
    wao_ref, wbo_ref, wo_ref, wpg_ref = _split_wsq(wsq_ref)
    i = pl.program_id(0)
    live = i < n_tiles
    t = jnp.minimum(i, n_tiles - 1) % n_t

    @pl.when(jnp.logical_and(t == 0, live))
    def _():
        sst_ref[...] = jnp.zeros(sst_ref.shape, F32)
        hist_ref[...] = jnp.zeros(hist_ref.shape, F32)

    @pl.when(i == 0)
    def _():
        st_yb_ref[...] = jnp.zeros(st_yb_ref.shape, BF16)
        st_ma_ref[...] = jnp.zeros(st_ma_ref.shape, F32)
        st_sgb_ref[...] = jnp.zeros(st_sgb_ref.shape, F32)

    x = x_ref[0]
    h = _rms(x, gpre_ref[...]).astype(BF16)
    q, k, la, v = _qkv_la(h, wa_ref, wr_ref, wup_ref, bup_ref)
    q_ref[...] = q
    k_ref[...] = k
    la_ref[...] = la
    v_ref[...] = v

    c = D_MODEL
    za0 = 2 * KEY_DIM + VAL_DIM

    def sliced(store, fn):
        return [functools.partial(lambda lo: store(lo, lo + MXU_SLICE, fn(lo, lo + MXU_SLICE)), n * MXU_SLICE)
                for n in range(c // MXU_SLICE)]

    def into(ref, *lead):
        def store(lo, hi, val):
            ref[(*lead, slice(None), slice(lo, hi))] = val
        return store

    def ticker(jobs, n_ticks):
        state = [0, 0]

        def tick():
            state[0] += 1
            upto = -(-len(jobs) * state[0] // n_ticks)
            while state[1] < min(upto, len(jobs)):
                jobs[state[1]]()
                state[1] += 1
        return tick

    pb = pb_ref[0].astype(BF16)
    back1 = sliced(into(mg_ref), lambda lo, hi: (
        st_ma_ref[:, lo:hi] + st_sgb_ref[:, lo:hi] * _dot(st_yb_ref[...], wbo_ref[:, lo:hi])).astype(BF16))

    def store_x1(lo, hi, val):
        x1_ref[:, lo:hi] = val
        x1b_ref[:, lo:hi] = val.astype(BF16)

    back2 = sliced(store_x1, lambda lo, hi: xb_ref[0, :, lo:hi] + _dot(mg_ref[...], wo_ref[:, lo:hi]))
    back3 = sliced(into(y_ref, 0), lambda lo, hi: (
        x1_ref[:, lo:hi] + _sigmoid(_dot(x1b_ref[...], wpg_ref[:, lo:hi])) * _dot(pb, wpe_ref[:, lo:hi])))

    n_chunks = tt // CHUNK
    jobs = (sliced(into(u_ref), lambda lo, hi: (_dot(h, wb_ref[:, lo:hi])
                                                * _sigmoid(_dot(h, wb_ref[:, c + lo:c + hi]))))
            + back1 + sliced(into(sza_ref), lambda lo, hi: _silu(_dot(h, wa_ref[:, za0 + lo:za0 + hi]))) + back2)
    tick = ticker(jobs, n_chunks * HEADS)
    for ci in range(n_chunks):
        rows = slice(ci * CHUNK, (ci + 1) * CHUNK)
        s_list = [sst_ref[hd] for hd in range(HEADS)]
        o, s_new = _gla_chunk(q_ref[rows, :], k_ref[rows, :], la_ref[rows, :], v_ref[rows, :],
                              s_list, tri_ref[...], emat_ref, mcat_ref, CHUNK, tick)
        o_ref[rows, :] = o
        for hd in range(HEADS):
            sst_ref[hd] = s_new[hd]

    ona = _gated_heads(o_ref[...], sza_ref[...], ggla_ref)
    jobs = (sliced(into(st_ma_ref), lambda lo, hi: _sigmoid(_dot(h, wb_ref[:, 3 * c + lo:3 * c + hi])))
            + back3
            + sliced(into(st_ma_ref), lambda lo, hi: st_ma_ref[:, lo:hi] * _dot(ona, wao_ref[:, lo:hi]))
            + sliced(into(szb_ref), lambda lo, hi: _silu(_dot(h, wb_ref[:, 2 * c + lo:2 * c + hi])))
            + sliced(into(st_sgb_ref), lambda lo, hi: _sigmoid(_dot(h, wb_ref[:, 4 * c + lo:4 * c + hi]))))
    tick = ticker(jobs, 2 * N_LB)
    _conv_tile(u_ref, xs_ref, hist_ref, cvs_ref, wdwb_ref, bdwb_ref, tt, tick)
    y_ref[0] = _rms(y_ref[0], gfin_ref[...])

    cv = jnp.concatenate([cvs_ref[cb] for cb in range(N_LB)], axis=1)
    st_yb_ref[...] = _conv_branch_in(cv, szb_ref[...], gln_ref, bln_ref)

    @pl.when(live)
    def _():
        s_ref[0] = sst_ref[...]
        cs_ref[0] = u_ref[tt - HIST:tt, :]


def _whole():
    return pl.BlockSpec(memory_space=pltpu.VMEM)


def _prompt_call(x, p, w, tt):
    bsz, t_len, _ = x.shape
    seg = tt // SUBLANES
    assert t_len % tt == 0 and tt % CHUNK == 0 and seg >= HIST and seg % SUBLANES == 0
    n_t = t_len // tt
    n_tiles = bsz * n_t
    front = lambda i: jnp.minimum(i, n_tiles - 1)
    back = lambda i: jnp.maximum(i - 1, 0)
    tok = lambda wd, tile: pl.BlockSpec((1, tt, wd), lambda i: (tile(i) // n_t, tile(i) % n_t, 0))
    names = ["gpre", "wa", "wb", "wr", "wup", "bup", "tri", "emat", "wdwb", "bdwb", "ggla", "gln", "bln", "wsq",
             "wpe", "gfin"]
    return pl.pallas_call(
        functools.partial(_prompt_kernel, tt=tt, n_t=n_t, n_tiles=n_tiles),
        grid=(n_tiles + 1,),
        in_specs=[tok(D_MODEL, front), tok(D_MODEL, back), tok(PLE, back)] + [_whole()] * len(names),
        out_specs=[tok(D_MODEL, back),
                   pl.BlockSpec((1, HEADS, DK, DV), lambda i: (front(i) // n_t, 0, 0, 0)),
                   pl.BlockSpec((1, HIST, D_MODEL), lambda i: (front(i) // n_t, 0, 0))],
        out_shape=[jax.ShapeDtypeStruct((bsz, t_len, D_MODEL), F32),
                   jax.ShapeDtypeStruct((bsz, HEADS, DK, DV), F32),
                   jax.ShapeDtypeStruct((bsz, HIST, D_MODEL), F32)],
        scratch_shapes=[pltpu.VMEM((tt, KEY_DIM), F32), pltpu.VMEM((tt, KEY_DIM), F32),
                        pltpu.VMEM((tt, KEY_DIM), F32), pltpu.VMEM((tt, VAL_DIM), BF16),
                        pltpu.VMEM((tt, VAL_DIM), F32),
                        pltpu.VMEM((HEADS * CHUNK, SUB * DK), BF16),
                        pltpu.VMEM((N_LB, (HIST + seg) * SUBLANES, LANES), F32),
                        pltpu.VMEM((N_LB, HIST * SUBLANES, LANES), F32),
                        pltpu.VMEM((N_LB, tt, LANES), F32),
                        pltpu.VMEM((HEADS, DK, DV), F32),
                        pltpu.VMEM((tt, D_MODEL), BF16),
                        pltpu.VMEM((tt, D_MODEL), F32), pltpu.VMEM((tt, D_MODEL), F32),
                        pltpu.VMEM((tt, D_MODEL), F32), pltpu.VMEM((tt, D_MODEL), F32),
                        pltpu.VMEM((tt, D_MODEL), F32), pltpu.VMEM((tt, D_MODEL), BF16),
                        pltpu.VMEM((tt, D_MODEL), F32), pltpu.VMEM((tt, D_MODEL), BF16)],
        compiler_params=pltpu.CompilerParams(dimension_semantics=("arbitrary",),
                                             vmem_limit_bytes=VMEM_LIMIT),
        name="prompt_layer",
    )(x, x, p, *[w[n] for n in names])


def _pre_kernel(x_ref, gpre_ref, wa_ref, wb_ref, wr_ref, wup_ref, bup_ref,
                q_ref, k_ref, la_ref, v_ref, u_ref):
    h = _rms(x_ref[...], gpre_ref[...]).astype(BF16)
    q_ref[...], k_ref[...], la_ref[...], v_ref[...] = _qkv_la(h, wa_ref, wr_ref, wup_ref, bup_ref)
    u_ref[...] = _glu(h, wb_ref)


def _pre_call(x2d, w, tm):
    n = x2d.shape[0]
    row = lambda wd: pl.BlockSpec((tm, wd), lambda i: (i, 0))
    outs = [(KEY_DIM, F32), (KEY_DIM, F32), (KEY_DIM, F32), (VAL_DIM, BF16), (D_MODEL, F32)]
    names = ["gpre", "wa", "wb", "wr", "wup", "bup"]
    return pl.pallas_call(
        _pre_kernel,
        grid=(n // tm,),
        in_specs=[row(D_MODEL)] + [_whole()] * len(names),
        out_specs=[row(wd) for wd, _ in outs],
        out_shape=[jax.ShapeDtypeStruct((n, wd), dt) for wd, dt in outs],
        compiler_params=pltpu.CompilerParams(dimension_semantics=("arbitrary",),
                                             vmem_limit_bytes=VMEM_LIMIT),
        name="sample_pre",
    )(x2d, *[w[n_] for n_ in names])


def _sample_conv_kernel(c0_ref, u_ref, wdwb_ref, bdwb_ref, cv_ref, cs_ref, *, t_len, nbs):
    ext = lambda d, rows, cs: c0_ref[d, rows, cs] if d < HIST else u_ref[d - HIST, rows, cs]
    for r0 in range(0, nbs, SUBLANES):
        rows = slice(r0, r0 + SUBLANES)
        for cb in range(N_LB):
            cs = slice(cb * LANES, (cb + 1) * LANES)
            acc = [bdwb_ref[:, cs] for _ in range(t_len)]
            for d in range(HIST + t_len):
                x = ext(d, rows, cs)
                for t in range(t_len):
                    j = d - t
                    if 0 <= j < CONV_W:
                        acc[t] = acc[t] + wdwb_ref[SUBLANES * j:SUBLANES * (j + 1), cs] * x
            for t in range(t_len):
                cv_ref[t, rows, cs] = acc[t]
    for d in range(HIST):
        cs_ref[d] = ext(d + t_len, slice(None), slice(None))


def _sample_conv_call(c0t, ut, w, nbs):
    _, bsz, _ = c0t.shape
    t_len = ut.shape[0]
    assert bsz % nbs == 0 and nbs % SUBLANES == 0
    blk = lambda rows: pl.BlockSpec((rows, nbs, D_MODEL), lambda i: (0, i, 0))
    return pl.pallas_call(
        functools.partial(_sample_conv_kernel, t_len=t_len, nbs=nbs),
        grid=(bsz // nbs,),
        in_specs=[blk(HIST), blk(t_len), _whole(), _whole()],
        out_specs=[blk(t_len), blk(HIST)],
        out_shape=[jax.ShapeDtypeStruct((t_len, bsz, D_MODEL), F32),
                   jax.ShapeDtypeStruct((HIST, bsz, D_MODEL), F32)],
        compiler_params=pltpu.CompilerParams(dimension_semantics=("arbitrary",),
                                             vmem_limit_bytes=VMEM_LIMIT),
        name="sample_conv",
    )(c0t, ut, w["wdwb"], w["bdwb"])


def _mix_sample_kernel(q_ref, k_ref, la_ref, v_ref, s0_ref, tri_ref, emat_ref,
                       o_ref, s_ref, mcat_ref, *, nb, t_len):
    pad = PACK_ROWS - t_len

    def padf(z):
        zf = z.astype(F32)
        return jnp.concatenate([zf, jnp.zeros((pad, z.shape[1]), F32)], axis=0).astype(z.dtype)

    def pair(m, carry):
        for slot in range(2):
            n = 2 * m + slot
            s_list = [s0_ref[n, hd] for hd in range(HEADS)]
            o, s_new = _gla_chunk(padf(q_ref[n]), padf(k_ref[n]), padf(la_ref[n]), padf(v_ref[n]),
                                  s_list, tri_ref[...], emat_ref, mcat_ref.at[slot], t_len)
            o_ref[n] = o[0:t_len]
            for hd in range(HEADS):
                s_ref[n, hd] = s_new[hd]
        return carry

    lax.fori_loop(0, nb // 2, pair, 0)


def _mix_sample_call(q, k, la, v, s0, w, nb):
    bsz, t_len, _ = q.shape
    assert bsz % nb == 0 and nb % 2 == 0 and t_len <= SUB
    tok = lambda wd: pl.BlockSpec((nb, t_len, wd), lambda i: (i, 0, 0))
    st = pl.BlockSpec((nb, HEADS, DK, DV), lambda i: (i, 0, 0, 0))
    return pl.pallas_call(
        functools.partial(_mix_sample_kernel, nb=nb, t_len=t_len),
        grid=(bsz // nb,),
        in_specs=[tok(KEY_DIM), tok(KEY_DIM), tok(KEY_DIM), tok(VAL_DIM), st, _whole(), _whole()],
        out_specs=[tok(VAL_DIM), st],
        out_shape=[jax.ShapeDtypeStruct((bsz, t_len, VAL_DIM), F32),
                   jax.ShapeDtypeStruct((bsz, HEADS, DK, DV), F32)],
        scratch_shapes=[pltpu.VMEM((2, HEADS * PACK_ROWS, SUB * DK), BF16)],
        compiler_params=pltpu.CompilerParams(dimension_semantics=("arbitrary",),
                                             vmem_limit_bytes=VMEM_LIMIT),
        name="sample_mix",
    )(q, k, la, v, s0, _tri_const(PACK_ROWS), w["emat"])


def _post_kernel(x_ref, o_ref, cv_ref, p_ref, gpre_ref, wa_ref, wb_ref,
                 ggla_ref, gln_ref, bln_ref, wsq_ref, wpe_ref, gfin_ref, y_ref):
    wao_ref, wbo_ref, wo_ref, wpg_ref = _split_wsq(wsq_ref)
    x = x_ref[...]
    c = D_MODEL
    h = _rms(x, gpre_ref[...]).astype(BF16)
    y_a = _dot(_gated_heads(o_ref[...], _z_a(h, wa_ref), ggla_ref), wao_ref[...])
    szb = _silu(_dot(h, wb_ref[:, 2 * c:3 * c]))
    sga = _sigmoid(_dot(h, wb_ref[:, 3 * c:4 * c]))
    sgb = _sigmoid(_dot(h, wb_ref[:, 4 * c:5 * c]))
    pe = _dot(p_ref[...].astype(BF16), wpe_ref[...])
    y_ref[...] = _tail(x, y_a, cv_ref[...], szb, sga, sgb, pe, gln_ref, bln_ref, wbo_ref, wo_ref, wpg_ref, gfin_ref)


def _post_call(x2d, o, cv, p2d, w, tm):
    n = x2d.shape[0]
    row = lambda wd: pl.BlockSpec((tm, wd), lambda i: (i, 0))
    names = ["gpre", "wa", "wb", "ggla", "gln", "bln", "wsq", "wpe", "gfin"]
    return pl.pallas_call(
        _post_kernel,
        grid=(n // tm,),
        in_specs=[row(D_MODEL)] * 3 + [row(PLE)] + [_whole()] * len(names),
        out_specs=row(D_MODEL),
        out_shape=jax.ShapeDtypeStruct((n, D_MODEL), F32),
        compiler_params=pltpu.CompilerParams(dimension_semantics=("arbitrary",),
                                             vmem_limit_bytes=VMEM_LIMIT),
        name="sample_post",
    )(x2d, o, cv, p2d, *[w[n_] for n_ in names])


def _sample_layer(x, p, s0, c0, w, tm, nb, nbs):
    bsz, t_len, _ = x.shape
    x2d = x.reshape(bsz * t_len, D_MODEL)
    q, k, la, v, u = _pre_call(x2d, w, tm)
    r3 = lambda z: z.reshape(bsz, t_len, z.shape[-1])
    tmaj = lambda z: jnp.transpose(z, (1, 0, 2))
    o, s_new = _mix_sample_call(r3(q), r3(k), r3(la), r3(v), s0, w, nb)
    cv_t, c_new_t = _sample_conv_call(tmaj(c0), tmaj(r3(u)), w, nbs)
    y = _post_call(x2d, o.reshape(-1, VAL_DIM), tmaj(cv_t).reshape(-1, D_MODEL), p.reshape(bsz * t_len, PLE), w, tm)
    return y.reshape(bsz, t_len, D_MODEL), s_new, tmaj(c_new_t)


def kernel(x_prompt, x_sample, state_gla, state_conv, p_prompt, p_sample, g_pre, w_in, w_a_up, b_a_up, g_gla, w_a_out, w_dw, b_dw, g_ln, b_ln, w_b_out, w_o, w_pe, w_pg, g_final):
    depth = w_in.shape[0]
    assert depth == 1, "one trunk layer"
    i = 0
    tl = _tiles()
    n_a = 2 * KEY_DIM + 2 * VAL_DIM
    row = lambda z: z.reshape(1, -1)
    w = {
        "gpre": row(g_pre[i]),
        "wa": w_in[i][:, :n_a].astype(BF16),
        "wb": w_in[i][:, n_a + RANK:].astype(BF16),
        "wr": jnp.pad(w_in[i][:, n_a:n_a + RANK], ((0, 0), (0, LANES - RANK))).astype(BF16),
        "wup": jnp.pad(w_a_up[i], ((0, LANES - RANK), (0, 0))).astype(BF16),
        "bup": row(b_a_up[i]),
        "tri": _tri_const(CHUNK), "emat": _emat_const(),
        "wdwb": jnp.repeat(w_dw[i], SUBLANES, axis=0),
        "bdwb": jnp.broadcast_to(row(b_dw[i]), (SUBLANES, D_MODEL)),
        "ggla": row(g_gla[i]), "gln": row(g_ln[i]), "bln": row(b_ln[i]),
        "wsq": jnp.stack([w_a_out[i], w_b_out[i], w_o[i], w_pg[i]]).astype(BF16),
        "wpe": w_pe[i].astype(BF16), "gfin": row(g_final),
    }
    yp, sp, cp = _prompt_call(x_prompt, p_prompt[i], w, tl["tt"])
    ys, ss, cs = _sample_layer(x_sample, p_sample[i], state_gla[i], state_conv[i], w, tl["tm"], tl["nb"],
                               tl["nbs"])
    return (yp, ys, sp[None], cp[None], ss[None], cs[None])
```

```python
import functools

import numpy as np
import jax
import jax.numpy as jnp
from jax import lax
from jax.experimental import pallas as pl
from jax.experimental.pallas import tpu as pltpu

F32 = jnp.float32
BF16 = jnp.bfloat16

D_MODEL = 1024
HEADS = 4
DK = 128
DV = 256
KEY_DIM = HEADS * DK
VAL_DIM = HEADS * DV
RANK = 16
TAU = 16.0
CHUNK = 64
SUB = 8
PACK_ROWS = 16
CONV_W = 31
HIST = CONV_W - 1
PLE = 256
EPS = 1e-6
LOG2E = 1.4426950408889634
LANES = 128
SUBLANES = 8
N_LB = D_MODEL // LANES
MXU_SLICE = 256
SEQ_UNROLL = 4
VMEM_LIMIT = 62 * 1024 * 1024


def _tiles():
    return dict(tt=256, tm=256, nb=8, nbs=16)


def _dot(a, b):
    return jnp.dot(a, b, preferred_element_type=F32)


def _sigmoid(x):
    return 1.0 / (1.0 + jnp.exp2(x * (-LOG2E)))


def _silu(x):
    return x * _sigmoid(x)


def _log_sigmoid(x):
    return -(jnp.maximum(-x, 0.0) + jnp.log(1.0 + jnp.exp(-jnp.abs(x))))


def _rms(x, g):
    return x * lax.rsqrt(jnp.mean(x * x, axis=-1, keepdims=True) + EPS) * g


def _qkv_la(h, wa_ref, wr_ref, wup_ref, bup_ref):
    q = _dot(h, wa_ref[:, 0:KEY_DIM]) * (DK ** -0.5)
    k = _dot(h, wa_ref[:, KEY_DIM:2 * KEY_DIM])
    v = _dot(h, wa_ref[:, 2 * KEY_DIM:2 * KEY_DIM + VAL_DIM]).astype(BF16)
    r = _dot(h, wr_ref[...]).astype(BF16)
    la = _log_sigmoid(_dot(r, wup_ref[...]) + bup_ref[...]) * (LOG2E / TAU)
    return q, k, la, v


def _z_a(h, wa_ref):
    return _silu(_dot(h, wa_ref[:, 2 * KEY_DIM + VAL_DIM:]))


def _glu(h, wb_ref):
    c = D_MODEL
    return _dot(h, wb_ref[:, 0:c]) * _sigmoid(_dot(h, wb_ref[:, c:2 * c]))


def _gated_heads(o, sza, ggla_ref):
    g = ggla_ref[...]
    on = jnp.concatenate([_rms(o[:, hd * DV:(hd + 1) * DV], g) for hd in range(HEADS)], axis=1)
    return (on * sza).astype(BF16)


def _conv_branch_in(cv, szb, gln_ref, bln_ref):
    mu = jnp.mean(cv, axis=-1, keepdims=True)
    xc = cv - mu
    ln = xc * lax.rsqrt(jnp.mean(xc * xc, axis=-1, keepdims=True) + EPS) * gln_ref[...] + bln_ref[...]
    return (_silu(ln) * szb).astype(BF16)


def _tail(x, y_a, cv, szb, sga, sgb, pe, gln_ref, bln_ref, wbo_ref, wo_ref, wpg_ref, gfin_ref):
    y_b = _dot(_conv_branch_in(cv, szb, gln_ref, bln_ref), wbo_ref[...])
    merged = sga * y_a + sgb * y_b
    x1 = x + _dot(merged.astype(BF16), wo_ref[...])
    x2 = x1 + _sigmoid(_dot(x1.astype(BF16), wpg_ref[...])) * pe
    return _rms(x2, gfin_ref[...])


def _gla_chunk(q, k, a, v, s_list, tri, emat, mcat_ref, n_keys, tick=lambda: None):
    c = q.shape[0]
    n_sub = c // SUB
    k_used = min(SUB, n_keys)
    a0 = a.astype(BF16)
    r1 = a - a0.astype(F32)
    a1 = r1.astype(BF16)
    a2 = (r1 - a1.astype(F32)).astype(BF16)
    b = _dot(tri, a0) + _dot(tri, a1) + _dot(tri, a2)

    att_off, q_dec, k_dec, d_last = [], [], [], []
    for h in range(HEADS):
        sl = slice(h * DK, (h + 1) * DK)
        qh, kh, bh = q[:, sl], k[:, sl], b[:, sl]
        bl = bh[c - 1:c, :]
        q_dec.append((qh * jnp.exp2(bh)).astype(BF16))
        k_dec.append(kh * jnp.exp2(bl - bh))
        d_last.append(jnp.exp2(bl))
        zrow = jnp.zeros((SUB, DK), F32)
        rows = [jnp.zeros((SUB, c), F32)]
        for i in range(1, n_sub):
            lo = i * SUB
            if lo >= n_keys:
                rows.append(jnp.zeros((SUB, c), F32))
                continue
            br = bh[lo - 1:lo, :]
            qi = jnp.concatenate([qh[lo:lo + SUB] * jnp.exp2(bh[lo:lo + SUB] - br), zrow], axis=0).astype(BF16)
            ki = kh[:lo] * jnp.exp2(br - bh[:lo])
            kfull = jnp.concatenate([ki, jnp.zeros((c - lo, DK), F32)], axis=0).astype(BF16)
            rows.append(lax.dot_general(qi, kfull, (((1,), (1,)), ((), ())),
                                        preferred_element_type=F32)[0:SUB])
        att_off.append(jnp.concatenate(rows, axis=0))
        for g0 in range(0, c, PACK_ROWS):
            for s in range(k_used):
                halves = []
                for lo in (g0, g0 + SUB):
                    if lo + s < n_keys:
                        qb, bb = qh[lo:lo + SUB], bh[lo:lo + SUB]
                        dec = jnp.exp2(jnp.minimum(bb - bh[lo + s:lo + s + 1, :], 0.0))
                        halves.append(qb * dec * kh[lo + s:lo + s + 1, :])
                    else:
                        halves.append(zrow)
                mcat_ref[h * c + g0:h * c + g0 + PACK_ROWS, s * DK:(s + 1) * DK] = (
                    jnp.concatenate(halves, axis=0).astype(BF16))
        tick()

    rsum = _dot(mcat_ref[:, 0:k_used * DK], emat[0:k_used * DK, :])
    ti = lax.broadcasted_iota(jnp.int32, (c, c), 0)
    si = lax.broadcasted_iota(jnp.int32, (c, c), 1)
    diag_mask = (ti // SUB == si // SUB) & (si <= ti)

    o_parts, s_new = [], []
    for h in range(HEADS):
        vh = v[:, h * DV:(h + 1) * DV]
        att = att_off[h] + jnp.where(diag_mask, rsum[h * c:(h + 1) * c, 0:c], 0.0)
        lhs = jnp.concatenate([q_dec[h], att.astype(BF16)], axis=1)
        rhs = jnp.concatenate([s_list[h].astype(BF16), vh], axis=0)
        o_parts.append(_dot(lhs, rhs))
        dcol = jnp.broadcast_to(d_last[h], (DK, DK)).T
        kt = k_dec[h].T.astype(BF16)
        s_new.append(jnp.concatenate([dcol, dcol], axis=1) * s_list[h] + _dot(kt, vh))
    return jnp.concatenate(o_parts, axis=1), s_new


def _tri_const(c):
    return jnp.asarray(np.tril(np.ones((c, c), np.float32)), BF16)


def _emat_const():
    e = np.zeros((SUB * DK, LANES), np.float32)
    rows = np.arange(SUB * DK)
    for l in range(LANES):
        e[rows[rows // DK == l % SUB], l] = 1.0
    return jnp.asarray(e, BF16)


def _conv_tile(u_ref, xs_ref, hist_ref, cvs_ref, wdwb_ref, bdwb_ref, tt, tick=lambda: None):
    seg = tt // SUBLANES
    x0 = HIST * SUBLANES
    sub7 = lax.broadcasted_iota(jnp.int32, (SUBLANES, LANES), 0) == SUBLANES - 1
    grp = SUBLANES
    for cb in range(N_LB):
        cs = slice(cb * LANES, (cb + 1) * LANES)
        for s in range(SUBLANES):
            for a0 in range(0, seg, SUBLANES):
                val = u_ref[seg * s + a0:seg * s + a0 + SUBLANES, cs]
                xs_ref[cb, pl.ds(x0 + SUBLANES * a0 + s, SUBLANES, stride=SUBLANES), :] = val
        for a in range(seg - HIST, seg):
            e = a - (seg - HIST)
            new = xs_ref[cb, x0 + SUBLANES * a:x0 + SUBLANES * (a + 1), :]
            old = hist_ref[cb, SUBLANES * e:SUBLANES * (e + 1), :]
            xs_ref[cb, SUBLANES * e:SUBLANES * (e + 1), :] = pltpu.roll(jnp.where(sub7, old, new), 1, axis=0)
            hist_ref[cb, SUBLANES * e:SUBLANES * (e + 1), :] = new
        w = [wdwb_ref[SUBLANES * j:SUBLANES * (j + 1), cs] for j in range(CONV_W)]
        for g0 in range(0, seg, grp):
            if g0 % (seg // 2) == 0:
                tick()
            acc = [bdwb_ref[:, cs] for _ in range(grp)]
            for d in range(g0, g0 + grp + HIST):
                x = xs_ref[cb, SUBLANES * d:SUBLANES * (d + 1), :]
                for i in range(grp):
                    j = d - (g0 + i)
                    if 0 <= j < CONV_W:
                        acc[i] = acc[i] + w[j] * x
            for i in range(grp):
                cvs_ref[cb, pl.ds(g0 + i, SUBLANES, stride=seg), :] = acc[i]


def _split_wsq(wsq_ref):
    return wsq_ref.at[0], wsq_ref.at[1], wsq_ref.at[2], wsq_ref.at[3]


def _prompt_kernel(x_ref, xb_ref, pb_ref, gpre_ref, wa_ref, wb_ref, wr_ref, wup_ref, bup_ref, tri_ref, emat_ref,
                   wdwb_ref, bdwb_ref, ggla_ref, gln_ref, bln_ref, wsq_ref, wpe_ref,
                   gfin_ref,
                   y_ref, s_ref, cs_ref,
                   q_ref, k_ref, la_ref, v_ref, o_ref, mcat_ref, xs_ref, hist_ref, cvs_ref, sst_ref,
                   st_yb_ref, st_ma_ref, st_sgb_ref, u_ref, sza_ref, szb_ref, mg_ref, x1_ref, x1b_ref,
                   *, tt, n_t, n_tiles):
    wao_ref, wbo_ref, wo_ref, wpg_ref = _split_wsq(wsq_ref)
    i = pl.program_id(0)
    live = i < n_tiles
    t = jnp.minimum(i, n_tiles - 1) % n_t

    @pl.when(jnp.logical_and(t == 0, live))
    def _():
        sst_ref[...] = jnp.zeros(sst_ref.shape, F32)
        hist_ref[...] = jnp.zeros(hist_ref.shape, F32)

    @pl.when(i == 0)
    def _():
        st_yb_ref[...] = jnp.zeros(st_yb_ref.shape, BF16)
        st_ma_ref[...] = jnp.zeros(st_ma_ref.shape, F32)
        st_sgb_ref[...] = jnp.zeros(st_sgb_ref.shape, F32)

    c = D_MODEL
    za0 = 2 * KEY_DIM + VAL_DIM

    def sliced(store, fn):
        return [functools.partial(lambda lo: store(lo, lo + MXU_SLICE, fn(lo, lo + MXU_SLICE)), n * MXU_SLICE)
                for n in range(c // MXU_SLICE)]

    def into(ref, *lead):
        def store(lo, hi, val):
            ref[(*lead, slice(None), slice(lo, hi))] = val
        return store

    def ticker(jobs, n_ticks):
        state = [0, 0]

        def tick():
            state[0] += 1
            upto = -(-len(jobs) * state[0] // n_ticks)
            while state[1] < min(upto, len(jobs)):
                jobs[state[1]]()
                state[1] += 1
        return tick

    def store_x1(lo, hi, val):
        x1_ref[:, lo:hi] = val
        x1b_ref[:, lo:hi] = val.astype(BF16)

    def back_jobs():
        pb = pb_ref[0].astype(BF16)
        back1 = sliced(into(mg_ref), lambda lo, hi: (
            st_ma_ref[:, lo:hi] + st_sgb_ref[:, lo:hi] * _dot(st_yb_ref[...], wbo_ref[:, lo:hi])).astype(BF16))
        back2 = sliced(store_x1, lambda lo, hi: xb_ref[0, :, lo:hi] + _dot(mg_ref[...], wo_ref[:, lo:hi]))
        back3 = sliced(into(y_ref, 0), lambda lo, hi: (
            x1_ref[:, lo:hi] + _sigmoid(_dot(x1b_ref[...], wpg_ref[:, lo:hi])) * _dot(pb, wpe_ref[:, lo:hi])))
        return back1, back2, back3

    @pl.when(live)
    def _():
        x = x_ref[0]
        h = _rms(x, gpre_ref[...]).astype(BF16)
        q_ref[...], k_ref[...], la_ref[...], v_ref[...] = _qkv_la(h, wa_ref, wr_ref, wup_ref, bup_ref)

        back1, back2, back3 = back_jobs()
        n_chunks = tt // CHUNK
        jobs = (sliced(into(u_ref), lambda lo, hi: (_dot(h, wb_ref[:, lo:hi])
                                                    * _sigmoid(_dot(h, wb_ref[:, c + lo:c + hi]))))
                + back1 + sliced(into(sza_ref), lambda lo, hi: _silu(_dot(h, wa_ref[:, za0 + lo:za0 + hi])))
                + back2)
        tick = ticker(jobs, n_chunks * HEADS)
        for ci in range(n_chunks):
            rows = slice(ci * CHUNK, (ci + 1) * CHUNK)
            s_list = [sst_ref[hd] for hd in range(HEADS)]
            o, s_new = _gla_chunk(q_ref[rows, :], k_ref[rows, :], la_ref[rows, :], v_ref[rows, :],
                                  s_list, tri_ref[...], emat_ref, mcat_ref, CHUNK, tick)
            o_ref[rows, :] = o
            for hd in range(HEADS):
                sst_ref[hd] = s_new[hd]

        ona = _gated_heads(o_ref[...], sza_ref[...], ggla_ref)
        jobs = (sliced(into(st_ma_ref), lambda lo, hi: _sigmoid(_dot(h, wb_ref[:, 3 * c + lo:3 * c + hi])))
                + back3
                + sliced(into(st_ma_ref), lambda lo, hi: st_ma_ref[:, lo:hi] * _dot(ona, wao_ref[:, lo:hi]))
                + sliced(into(szb_ref), lambda lo, hi: _silu(_dot(h, wb_ref[:, 2 * c + lo:2 * c + hi])))
                + sliced(into(st_sgb_ref), lambda lo, hi: _sigmoid(_dot(h, wb_ref[:, 4 * c + lo:4 * c + hi]))))
        tick = ticker(jobs, 2 * N_LB)
        _conv_tile(u_ref, xs_ref, hist_ref, cvs_ref, wdwb_ref, bdwb_ref, tt, tick)
        y_ref[0] = _rms(y_ref[0], gfin_ref[...])

        cv = jnp.concatenate([cvs_ref[cb] for cb in range(N_LB)], axis=1)
        st_yb_ref[...] = _conv_branch_in(cv, szb_ref[...], gln_ref, bln_ref)
        s_ref[0] = sst_ref[...]
        cs_ref[0] = u_ref[tt - HIST:tt, :]

    @pl.when(jnp.logical_not(live))
    def _():
        for job in sum(back_jobs(), []):
            job()
        y_ref[0] = _rms(y_ref[0], gfin_ref[...])


def _whole():
    return pl.BlockSpec(memory_space=pltpu.VMEM)


def _prompt_call(x, p, w, tt):
    bsz, t_len, _ = x.shape
    seg = tt // SUBLANES
    assert t_len % tt == 0 and tt % CHUNK == 0 and seg >= HIST and seg % SUBLANES == 0
    n_t = t_len // tt
    n_tiles = bsz * n_t
    front = lambda i: jnp.minimum(i, n_tiles - 1)
    back = lambda i: jnp.maximum(i - 1, 0)
    tok = lambda wd, tile: pl.BlockSpec((1, tt, wd), lambda i: (tile(i) // n_t, tile(i) % n_t, 0))
    names = ["gpre", "wa", "wb", "wr", "wup", "bup", "tri", "emat", "wdwb", "bdwb", "ggla", "gln", "bln", "wsq",
             "wpe", "gfin"]
    return pl.pallas_call(
        functools.partial(_prompt_kernel, tt=tt, n_t=n_t, n_tiles=n_tiles),
        grid=(n_tiles + 1,),
        in_specs=[tok(D_MODEL, front), tok(D_MODEL, back), tok(PLE, back)] + [_whole()] * len(names),
        out_specs=[tok(D_MODEL, back),
                   pl.BlockSpec((1, HEADS, DK, DV), lambda i: (front(i) // n_t, 0, 0, 0)),
                   pl.BlockSpec((1, HIST, D_MODEL), lambda i: (front(i) // n_t, 0, 0))],
        out_shape=[jax.ShapeDtypeStruct((bsz, t_len, D_MODEL), F32),
                   jax.ShapeDtypeStruct((bsz, HEADS, DK, DV), F32),
                   jax.ShapeDtypeStruct((bsz, HIST, D_MODEL), F32)],
        scratch_shapes=[pltpu.VMEM((tt, KEY_DIM), F32), pltpu.VMEM((tt, KEY_DIM), F32),
                        pltpu.VMEM((tt, KEY_DIM), F32), pltpu.VMEM((tt, VAL_DIM), BF16),
                        pltpu.VMEM((tt, VAL_DIM), F32),
                        pltpu.VMEM((HEADS * CHUNK, SUB * DK), BF16),
                        pltpu.VMEM((N_LB, (HIST + seg) * SUBLANES, LANES), F32),
                        pltpu.VMEM((N_LB, HIST * SUBLANES, LANES), F32),
                        pltpu.VMEM((N_LB, tt, LANES), F32),
                        pltpu.VMEM((HEADS, DK, DV), F32),
                        pltpu.VMEM((tt, D_MODEL), BF16),
                        pltpu.VMEM((tt, D_MODEL), F32), pltpu.VMEM((tt, D_MODEL), F32),
                        pltpu.VMEM((tt, D_MODEL), F32), pltpu.VMEM((tt, D_MODEL), F32),
                        pltpu.VMEM((tt, D_MODEL), F32), pltpu.VMEM((tt, D_MODEL), BF16),
                        pltpu.VMEM((tt, D_MODEL), F32), pltpu.VMEM((tt, D_MODEL), BF16)],
        compiler_params=pltpu.CompilerParams(dimension_semantics=("arbitrary",),
                                             vmem_limit_bytes=VMEM_LIMIT),
        name="prompt_layer",
    )(x, x, p, *[w[n] for n in names])


def _pre_kernel(x_ref, gpre_ref, wa_ref, wb_ref, wr_ref, wup_ref, bup_ref,
                q_ref, k_ref, la_ref, v_ref, u_ref):
    h = _rms(x_ref[...], gpre_ref[...]).astype(BF16)
    q_ref[...], k_ref[...], la_ref[...], v_ref[...] = _qkv_la(h, wa_ref, wr_ref, wup_ref, bup_ref)
    u_ref[...] = _glu(h, wb_ref)


def _pre_call(x2d, w, tm):
    n = x2d.shape[0]
    row = lambda wd: pl.BlockSpec((tm, wd), lambda i: (i, 0))
    outs = [(KEY_DIM, F32), (KEY_DIM, F32), (KEY_DIM, F32), (VAL_DIM, BF16), (D_MODEL, F32)]
    names = ["gpre", "wa", "wb", "wr", "wup", "bup"]
    return pl.pallas_call(
        _pre_kernel,
        grid=(n // tm,),
        in_specs=[row(D_MODEL)] + [_whole()] * len(names),
        out_specs=[row(wd) for wd, _ in outs],
        out_shape=[jax.ShapeDtypeStruct((n, wd), dt) for wd, dt in outs],
        compiler_params=pltpu.CompilerParams(dimension_semantics=("arbitrary",),
                                             vmem_limit_bytes=VMEM_LIMIT),
        name="sample_pre",
    )(x2d, *[w[n_] for n_ in names])


def _sample_conv_kernel(c0_ref, u_ref, wdwb_ref, bdwb_ref, cv_ref, cs_ref, *, t_len, nbs):
    ext = lambda d, rows, cs: c0_ref[d, rows, cs] if d < HIST else u_ref[d - HIST, rows, cs]
    for r0 in range(0, nbs, SUBLANES):
        rows = slice(r0, r0 + SUBLANES)
        for cb in range(N_LB):
            cs = slice(cb * LANES, (cb + 1) * LANES)
            acc = [bdwb_ref[:, cs] for _ in range(t_len)]
            for d in range(HIST + t_len):
                x = ext(d, rows, cs)
                for t in range(t_len):
                    j = d - t
                    if 0 <= j < CONV_W:
                        acc[t] = acc[t] + wdwb_ref[SUBLANES * j:SUBLANES * (j + 1), cs] * x
            for t in range(t_len):
                cv_ref[t, rows, cs] = acc[t]
    for d in range(HIST):
        cs_ref[d] = ext(d + t_len, slice(None), slice(None))


def _sample_conv_call(c0t, ut, w, nbs):
    _, bsz, _ = c0t.shape
    t_len = ut.shape[0]
    assert bsz % nbs == 0 and nbs % SUBLANES == 0
    blk = lambda rows: pl.BlockSpec((rows, nbs, D_MODEL), lambda i: (0, i, 0))
    return pl.pallas_call(
        functools.partial(_sample_conv_kernel, t_len=t_len, nbs=nbs),
        grid=(bsz // nbs,),
        in_specs=[blk(HIST), blk(t_len), _whole(), _whole()],
        out_specs=[blk(t_len), blk(HIST)],
        out_shape=[jax.ShapeDtypeStruct((t_len, bsz, D_MODEL), F32),
                   jax.ShapeDtypeStruct((HIST, bsz, D_MODEL), F32)],
        compiler_params=pltpu.CompilerParams(dimension_semantics=("arbitrary",),
                                             vmem_limit_bytes=VMEM_LIMIT),
        name="sample_conv",
    )(c0t, ut, w["wdwb"], w["bdwb"])


def _mix_sample_kernel(q_ref, k_ref, la_ref, v_ref, s0_ref, tri_ref, emat_ref,
                       o_ref, s_ref, mcat_ref, *, nb, t_len):
    pad = PACK_ROWS - t_len

    def padf(z):
        zf = z.astype(F32)
        return jnp.concatenate([zf, jnp.zeros((pad, z.shape[1]), F32)], axis=0).astype(z.dtype)

    def pair(m, carry):
        for slot in range(SEQ_UNROLL):
            n = SEQ_UNROLL * m + slot
            s_list = [s0_ref[n, hd] for hd in range(HEADS)]
            o, s_new = _gla_chunk(padf(q_ref[n]), padf(k_ref[n]), padf(la_ref[n]), padf(v_ref[n]),
                                  s_list, tri_ref[...], emat_ref, mcat_ref.at[slot], t_len)
            o_ref[n] = o[0:t_len]
            for hd in range(HEADS):
                s_ref[n, hd] = s_new[hd]
        return carry

    lax.fori_loop(0, nb // SEQ_UNROLL, pair, 0)


def _mix_sample_call(q, k, la, v, s0, w, nb):
    bsz, t_len, _ = q.shape
    assert bsz % nb == 0 and nb % SEQ_UNROLL == 0 and t_len <= SUB
    tok = lambda wd: pl.BlockSpec((nb, t_len, wd), lambda i: (i, 0, 0))
    st = pl.BlockSpec((nb, HEADS, DK, DV), lambda i: (i, 0, 0, 0))
    return pl.pallas_call(
        functools.partial(_mix_sample_kernel, nb=nb, t_len=t_len),
        grid=(bsz // nb,),
        in_specs=[tok(KEY_DIM), tok(KEY_DIM), tok(KEY_DIM), tok(VAL_DIM), st, _whole(), _whole()],
        out_specs=[tok(VAL_DIM), st],
        out_shape=[jax.ShapeDtypeStruct((bsz, t_len, VAL_DIM), F32),
                   jax.ShapeDtypeStruct((bsz, HEADS, DK, DV), F32)],
        scratch_shapes=[pltpu.VMEM((SEQ_UNROLL, HEADS * PACK_ROWS, SUB * DK), BF16)],
        compiler_params=pltpu.CompilerParams(dimension_semantics=("arbitrary",),
                                             vmem_limit_bytes=VMEM_LIMIT),
        name="sample_mix",
    )(q, k, la, v, s0, _tri_const(PACK_ROWS), w["emat"])


def _post_kernel(x_ref, o_ref, cv_ref, p_ref, gpre_ref, wa_ref, wb_ref,
                 ggla_ref, gln_ref, bln_ref, wsq_ref, wpe_ref, gfin_ref, y_ref):
    wao_ref, wbo_ref, wo_ref, wpg_ref = _split_wsq(wsq_ref)
    x = x_ref[...]
    c = D_MODEL
    h = _rms(x, gpre_ref[...]).astype(BF16)
    y_a = _dot(_gated_heads(o_ref[...], _z_a(h, wa_ref), ggla_ref), wao_ref[...])
    szb = _silu(_dot(h, wb_ref[:, 2 * c:3 * c]))
    sga = _sigmoid(_dot(h, wb_ref[:, 3 * c:4 * c]))
    sgb = _sigmoid(_dot(h, wb_ref[:, 4 * c:5 * c]))
    pe = _dot(p_ref[...].astype(BF16), wpe_ref[...])
    y_ref[...] = _tail(x, y_a, cv_ref[...], szb, sga, sgb, pe, gln_ref, bln_ref, wbo_ref, wo_ref, wpg_ref, gfin_ref)


def _post_call(x2d, o, cv, p2d, w, tm):
    n = x2d.shape[0]
    row = lambda wd: pl.BlockSpec((tm, wd), lambda i: (i, 0))
    names = ["gpre", "wa", "wb", "ggla", "gln", "bln", "wsq", "wpe", "gfin"]
    return pl.pallas_call(
        _post_kernel,
        grid=(n // tm,),
        in_specs=[row(D_MODEL)] * 3 + [row(PLE)] + [_whole()] * len(names),
        out_specs=row(D_MODEL),
        out_shape=jax.ShapeDtypeStruct((n, D_MODEL), F32),
        compiler_params=pltpu.CompilerParams(dimension_semantics=("arbitrary",),
                                             vmem_limit_bytes=VMEM_LIMIT),
        name="sample_post",
    )(x2d, o, cv, p2d, *[w[n_] for n_ in names])


def _sample_layer(x, p, s0, c0, w, tm, nb, nbs):
    bsz, t_len, _ = x.shape
    x2d = x.reshape(bsz * t_len, D_MODEL)
    q, k, la, v, u = _pre_call(x2d, w, tm)
    r3 = lambda z: z.reshape(bsz, t_len, z.shape[-1])
    tmaj = lambda z: jnp.transpose(z, (1, 0, 2))
    o, s_new = _mix_sample_call(r3(q), r3(k), r3(la), r3(v), s0, w, nb)
    cv_t, c_new_t = _sample_conv_call(tmaj(c0), tmaj(r3(u)), w, nbs)
    y = _post_call(x2d, o.reshape(-1, VAL_DIM), tmaj(cv_t).reshape(-1, D_MODEL), p.reshape(bsz * t_len, PLE), w, tm)
    return y.reshape(bsz, t_len, D_MODEL), s_new, tmaj(c_new_t)


def kernel(x_prompt, x_sample, state_gla, state_conv, p_prompt, p_sample, g_pre, w_in, w_a_up, b_a_up, g_gla, w_a_out, w_dw, b_dw, g_ln, b_ln, w_b_out, w_o, w_pe, w_pg, g_final):
    depth = w_in.shape[0]
    assert depth == 1, "one trunk layer"
    i = 0
    tl = _tiles()
    n_a = 2 * KEY_DIM + 2 * VAL_DIM
    row = lambda z: z.reshape(1, -1)
    w = {
        "gpre": row(g_pre[i]),
        "wa": w_in[i][:, :n_a].astype(BF16),
        "wb": w_in[i][:, n_a + RANK:].astype(BF16),
        "wr": jnp.pad(w_in[i][:, n_a:n_a + RANK], ((0, 0), (0, LANES - RANK))).astype(BF16),
        "wup": jnp.pad(w_a_up[i], ((0, LANES - RANK), (0, 0))).astype(BF16),
        "bup": row(b_a_up[i]),
        "tri": _tri_const(CHUNK), "emat": _emat_const(),
        "wdwb": jnp.repeat(w_dw[i], SUBLANES, axis=0),
        "bdwb": jnp.broadcast_to(row(b_dw[i]), (SUBLANES, D_MODEL)),
        "ggla": row(g_gla[i]), "gln": row(g_ln[i]), "bln": row(b_ln[i]),
        "wsq": jnp.stack([w_a_out[i], w_b_out[i], w_o[i], w_pg[i]]).astype(BF16),
        "wpe": w_pe[i].astype(BF16), "gfin": row(g_final),
    }
    yp, sp, cp = _prompt_call(x_prompt, p_prompt[i], w, tl["tt"])
    ys, ss, cs = _sample_layer(x_sample, p_sample[i], state_gla[i], state_conv[i], w, tl["tm"], tl["nb"],
                               tl["nbs"])
    return (yp, ys, sp[None], cp[None], ss[None], cs[None])
```

```python
import functools

import numpy as np
import jax
import jax.numpy as jnp
from jax import lax
from jax.experimental import pallas as pl
from jax.experimental.pallas import tpu as pltpu

F32 = jnp.float32
BF16 = jnp.bfloat16

D_MODEL = 1024
HEADS = 4
DK = 128
DV = 256
KEY_DIM = HEADS * DK
VAL_DIM = HEADS * DV
RANK = 16
TAU = 16.0
CHUNK = 64
SUB = 8
PACK_ROWS = 16
CONV_W = 31
HIST = CONV_W - 1
PLE = 256
EPS = 1e-6
LOG2E = 1.4426950408889634
LANES = 128
SUBLANES = 8
N_LB = D_MODEL // LANES
MXU_SLICE = 256
SEQ_UNROLL = 4
VMEM_LIMIT = 62 * 1024 * 1024


def _tiles():
    return dict(tt=256, tm=256, nb=8, nbs=16)


def _dot(a, b):
    return jnp.dot(a, b, preferred_element_type=F32)


def _sigmoid(x):
    return 1.0 / (1.0 + jnp.exp2(x * (-LOG2E)))


def _silu(x):
    return x * _sigmoid(x)


def _log_sigmoid(x):
    return -(jnp.maximum(-x, 0.0) + jnp.log(1.0 + jnp.exp(-jnp.abs(x))))


def _rms(x, g):
    return x * lax.rsqrt(jnp.mean(x * x, axis=-1, keepdims=True) + EPS) * g


def _qkv_la(h, wa_ref, wr_ref, wup_ref, bup_ref):
    q = _dot(h, wa_ref[:, 0:KEY_DIM]) * (DK ** -0.5)
    k = _dot(h, wa_ref[:, KEY_DIM:2 * KEY_DIM])
    v = _dot(h, wa_ref[:, 2 * KEY_DIM:2 * KEY_DIM + VAL_DIM]).astype(BF16)
    r = _dot(h, wr_ref[...]).astype(BF16)
    la = _log_sigmoid(_dot(r, wup_ref[...]) + bup_ref[...]) * (LOG2E / TAU)
    return q, k, la, v


def _z_a(h, wa_ref):
    return _silu(_dot(h, wa_ref[:, 2 * KEY_DIM + VAL_DIM:]))


def _glu(h, wb_ref):
    c = D_MODEL
    return _dot(h, wb_ref[:, 0:c]) * _sigmoid(_dot(h, wb_ref[:, c:2 * c]))


def _gated_heads(o, sza, ggla_ref):
    g = ggla_ref[...]
    on = jnp.concatenate([_rms(o[:, hd * DV:(hd + 1) * DV], g) for hd in range(HEADS)], axis=1)
    return (on * sza).astype(BF16)


def _conv_branch_in(cv, szb, gln_ref, bln_ref):
    mu = jnp.mean(cv, axis=-1, keepdims=True)
    xc = cv - mu
    ln = xc * lax.rsqrt(jnp.mean(xc * xc, axis=-1, keepdims=True) + EPS) * gln_ref[...] + bln_ref[...]
    return (_silu(ln) * szb).astype(BF16)


def _tail(x, y_a, cv, szb, sga, sgb, pe, gln_ref, bln_ref, wbo_ref, wo_ref, wpg_ref, gfin_ref):
    y_b = _dot(_conv_branch_in(cv, szb, gln_ref, bln_ref), wbo_ref[...])
    merged = sga * y_a + sgb * y_b
    x1 = x + _dot(merged.astype(BF16), wo_ref[...])
    x2 = x1 + _sigmoid(_dot(x1.astype(BF16), wpg_ref[...])) * pe
    return _rms(x2, gfin_ref[...])


def _gla_chunk(q, k, a, v, s_list, tri, emat, mcat_ref, n_keys, tick=lambda: None):
    c = q.shape[0]
    n_sub = c // SUB
    k_used = min(SUB, n_keys)
    a0 = a.astype(BF16)
    r1 = a - a0.astype(F32)
    a1 = r1.astype(BF16)
    a2 = (r1 - a1.astype(F32)).astype(BF16)
    b = _dot(tri, a0) + _dot(tri, a1) + _dot(tri, a2)

    att_off, q_dec, k_dec, d_last = [], [], [], []
    for h in range(HEADS):
        sl = slice(h * DK, (h + 1) * DK)
        qh, kh, bh = q[:, sl], k[:, sl], b[:, sl]
        bl = bh[c - 1:c, :]
        q_dec.append((qh * jnp.exp2(bh)).astype(BF16))
        k_dec.append(kh * jnp.exp2(bl - bh))
        d_last.append(jnp.exp2(bl))
        zrow = jnp.zeros((SUB, DK), F32)
        rows = [jnp.zeros((SUB, c), F32)]
        for i in range(1, n_sub):
            lo = i * SUB
            if lo >= n_keys:
                rows.append(jnp.zeros((SUB, c), F32))
                continue
            br = bh[lo - 1:lo, :]
            qi = jnp.concatenate([qh[lo:lo + SUB] * jnp.exp2(bh[lo:lo + SUB] - br), zrow], axis=0).astype(BF16)
            ki = kh[:lo] * jnp.exp2(br - bh[:lo])
            kfull = jnp.concatenate([ki, jnp.zeros((c - lo, DK), F32)], axis=0).astype(BF16)
            rows.append(lax.dot_general(qi, kfull, (((1,), (1,)), ((), ())),
                                        preferred_element_type=F32)[0:SUB])
        att_off.append(jnp.concatenate(rows, axis=0))
        for g0 in range(0, c, PACK_ROWS):
            for s in range(k_used):
                halves = []
                for lo in (g0, g0 + SUB):
                    if lo + s < n_keys:
                        qb, bb = qh[lo:lo + SUB], bh[lo:lo + SUB]
                        dec = jnp.exp2(jnp.minimum(bb - bh[lo + s:lo + s + 1, :], 0.0))
                        halves.append(qb * dec * kh[lo + s:lo + s + 1, :])
                    else:
                        halves.append(zrow)
                mcat_ref[h * c + g0:h * c + g0 + PACK_ROWS, s * DK:(s + 1) * DK] = (
                    jnp.concatenate(halves, axis=0).astype(BF16))
        tick()

    rsum = _dot(mcat_ref[:, 0:k_used * DK], emat[0:k_used * DK, :])
    ti = lax.broadcasted_iota(jnp.int32, (c, c), 0)
    si = lax.broadcasted_iota(jnp.int32, (c, c), 1)
    diag_mask = (ti // SUB == si // SUB) & (si <= ti)

    o_parts, s_new = [], []
    for h in range(HEADS):
        vh = v[:, h * DV:(h + 1) * DV]
        att = att_off[h] + jnp.where(diag_mask, rsum[h * c:(h + 1) * c, 0:c], 0.0)
        lhs = jnp.concatenate([q_dec[h], att.astype(BF16)], axis=1)
        rhs = jnp.concatenate([s_list[h].astype(BF16), vh], axis=0)
        o_parts.append(_dot(lhs, rhs))
        dcol = jnp.broadcast_to(d_last[h], (DK, DK)).T
        kt = k_dec[h].T.astype(BF16)
        s_new.append(jnp.concatenate([dcol, dcol], axis=1) * s_list[h] + _dot(kt, vh))
    return jnp.concatenate(o_parts, axis=1), s_new


def _tri_const(c):
    return jnp.asarray(np.tril(np.ones((c, c), np.float32)), BF16)


def _emat_const():
    e = np.zeros((SUB * DK, LANES), np.float32)
    rows = np.arange(SUB * DK)
    for l in range(LANES):
        e[rows[rows // DK == l % SUB], l] = 1.0
    return jnp.asarray(e, BF16)


def _conv_tile(u_ref, xs_ref, hist_ref, cvs_ref, wdwb_ref, bdwb_ref, tt, tick=lambda: None):
    seg = tt // SUBLANES
    x0 = HIST * SUBLANES
    sub7 = lax.broadcasted_iota(jnp.int32, (SUBLANES, LANES), 0) == SUBLANES - 1
    grp = SUBLANES
    for cb in range(N_LB):
        cs = slice(cb * LANES, (cb + 1) * LANES)
        for s in range(SUBLANES):
            for a0 in range(0, seg, SUBLANES):
                val = u_ref[seg * s + a0:seg * s + a0 + SUBLANES, cs]
                xs_ref[cb, pl.ds(x0 + SUBLANES * a0 + s, SUBLANES, stride=SUBLANES), :] = val
        for a in range(seg - HIST, seg):
            e = a - (seg - HIST)
            new = xs_ref[cb, x0 + SUBLANES * a:x0 + SUBLANES * (a + 1), :]
            old = hist_ref[cb, SUBLANES * e:SUBLANES * (e + 1), :]
            xs_ref[cb, SUBLANES * e:SUBLANES * (e + 1), :] = pltpu.roll(jnp.where(sub7, old, new), 1, axis=0)
            hist_ref[cb, SUBLANES * e:SUBLANES * (e + 1), :] = new
        w = [wdwb_ref[SUBLANES * j:SUBLANES * (j + 1), cs] for j in range(CONV_W)]
        for g0 in range(0, seg, grp):
            if g0 % (seg // 2) == 0:
                tick()
            acc = [bdwb_ref[:, cs] for _ in range(grp)]
            for d in range(g0, g0 + grp + HIST):
                x = xs_ref[cb, SUBLANES * d:SUBLANES * (d + 1), :]
                for i in range(grp):
                    j = d - (g0 + i)
                    if 0 <= j < CONV_W:
                        acc[i] = acc[i] + w[j] * x
            for i in range(grp):
                cvs_ref[cb, pl.ds(g0 + i, SUBLANES, stride=seg), :] = acc[i]


def _split_wsq(wsq_ref):
    return wsq_ref.at[0], wsq_ref.at[1], wsq_ref.at[2], wsq_ref.at[3]


def _prompt_kernel(xn_ref, xb_ref, pb_ref, gpre_ref, wa_ref, wb_ref, wr_ref, wup_ref, bup_ref, tri_ref, emat_ref,
                   wdwb_ref, bdwb_ref, ggla_ref, gln_ref, bln_ref, wsq_ref, wpe_ref,
                   gfin_ref,
                   y_ref, s_ref, cs_ref,
                   q_ref, k_ref, la_ref, v_ref, h_ref, o_ref, mcat_ref, xs_ref, hist_ref, cvs_ref, sst_ref,
                   st_yb_ref, st_ma_ref, st_sgb_ref, u_ref, sza_ref, szb_ref, mg_ref, x1_ref, x1b_ref,
                   *, tt, n_t, n_tiles):
    wao_ref, wbo_ref, wo_ref, wpg_ref = _split_wsq(wsq_ref)
    i = pl.program_id(0)
    live = i < n_tiles
    t = jnp.minimum(i, n_tiles - 1) % n_t
    cur = i % 2
    nxt = 1 - cur

    @pl.when(jnp.logical_and(t == 0, live))
    def _():
        sst_ref[...] = jnp.zeros(sst_ref.shape, F32)
        hist_ref[...] = jnp.zeros(hist_ref.shape, F32)

    @pl.when(i == 0)
    def _():
        st_yb_ref[...] = jnp.zeros(st_yb_ref.shape, BF16)
        st_ma_ref[...] = jnp.zeros(st_ma_ref.shape, F32)
        st_sgb_ref[...] = jnp.zeros(st_sgb_ref.shape, F32)
        h0 = _rms(xb_ref[0], gpre_ref[...]).astype(BF16)
        h_ref[0] = h0
        q_ref[0], k_ref[0], la_ref[0], v_ref[0] = _qkv_la(h0, wa_ref, wr_ref, wup_ref, bup_ref)

    c = D_MODEL
    za0 = 2 * KEY_DIM + VAL_DIM

    def sliced(store, fn, width=D_MODEL):
        return [functools.partial(lambda lo: store(lo, lo + MXU_SLICE, fn(lo, lo + MXU_SLICE)), n * MXU_SLICE)
                for n in range(width // MXU_SLICE)]

    def into(ref, *lead):
        def store(lo, hi, val):
            ref[(*lead, slice(None), slice(lo, hi))] = val
        return store

    def ticker(jobs, n_ticks):
        state = [0, 0]

        def tick():
            state[0] += 1
            upto = -(-len(jobs) * state[0] // n_ticks)
            while state[1] < min(upto, len(jobs)):
                jobs[state[1]]()
                state[1] += 1
        return tick

    def store_x1(lo, hi, val):
        x1_ref[:, lo:hi] = val
        x1b_ref[:, lo:hi] = val.astype(BF16)

    def back_jobs():
        pb = pb_ref[0].astype(BF16)
        back1 = sliced(into(mg_ref), lambda lo, hi: (
            st_ma_ref[:, lo:hi] + st_sgb_ref[:, lo:hi] * _dot(st_yb_ref[...], wbo_ref[:, lo:hi])).astype(BF16))
        back2 = sliced(store_x1, lambda lo, hi: xb_ref[0, :, lo:hi] + _dot(mg_ref[...], wo_ref[:, lo:hi]))
        back3 = sliced(into(y_ref, 0), lambda lo, hi: (
            x1_ref[:, lo:hi] + _sigmoid(_dot(x1b_ref[...], wpg_ref[:, lo:hi])) * _dot(pb, wpe_ref[:, lo:hi])))
        return back1, back2, back3

    @pl.when(live)
    def _():
        h = h_ref[cur]
        hn = _rms(xn_ref[0], gpre_ref[...]).astype(BF16)
        h_ref[nxt] = hn

        def store_la(lo, hi, r):
            la_ref[nxt] = _log_sigmoid(_dot(r.astype(BF16), wup_ref[...]) + bup_ref[...]) * (LOG2E / TAU)

        ahead = (sliced(into(q_ref, nxt), lambda lo, hi: _dot(hn, wa_ref[:, lo:hi]) * (DK ** -0.5), KEY_DIM)
                 + sliced(into(k_ref, nxt), lambda lo, hi: _dot(hn, wa_ref[:, KEY_DIM + lo:KEY_DIM + hi]),
                          KEY_DIM)
                 + sliced(into(v_ref, nxt), lambda lo, hi: _dot(
                     hn, wa_ref[:, 2 * KEY_DIM + lo:2 * KEY_DIM + hi]).astype(BF16), VAL_DIM)
                 + sliced(store_la, lambda lo, hi: _dot(hn, wr_ref[...]), MXU_SLICE))

        back1, back2, back3 = back_jobs()
        n_chunks = tt // CHUNK
        jobs = (sliced(into(u_ref), lambda lo, hi: (_dot(h, wb_ref[:, lo:hi])
                                                    * _sigmoid(_dot(h, wb_ref[:, c + lo:c + hi]))))
                + back1 + sliced(into(sza_ref), lambda lo, hi: _silu(_dot(h, wa_ref[:, za0 + lo:za0 + hi])))
                + back2 + ahead)
        tick = ticker(jobs, n_chunks * HEADS)
        for ci in range(n_chunks):
            rows = slice(ci * CHUNK, (ci + 1) * CHUNK)
            s_list = [sst_ref[hd] for hd in range(HEADS)]
            o, s_new = _gla_chunk(q_ref[cur, rows, :], k_ref[cur, rows, :], la_ref[cur, rows, :],
                                  v_ref[cur, rows, :], s_list, tri_ref[...], emat_ref, mcat_ref, CHUNK, tick)
            o_ref[rows, :] = o
            for hd in range(HEADS):
                sst_ref[hd] = s_new[hd]

        ona = _gated_heads(o_ref[...], sza_ref[...], ggla_ref)
        jobs = (sliced(into(st_ma_ref), lambda lo, hi: _sigmoid(_dot(h, wb_ref[:, 3 * c + lo:3 * c + hi])))
                + back3
                + sliced(into(st_ma_ref), lambda lo, hi: st_ma_ref[:, lo:hi] * _dot(ona, wao_ref[:, lo:hi]))
                + sliced(into(szb_ref), lambda lo, hi: _silu(_dot(h, wb_ref[:, 2 * c + lo:2 * c + hi])))
                + sliced(into(st_sgb_ref), lambda lo, hi: _sigmoid(_dot(h, wb_ref[:, 4 * c + lo:4 * c + hi]))))
        tick = ticker(jobs, 2 * N_LB)
        _conv_tile(u_ref, xs_ref, hist_ref, cvs_ref, wdwb_ref, bdwb_ref, tt, tick)
        y_ref[0] = _rms(y_ref[0], gfin_ref[...])

        cv = jnp.concatenate([cvs_ref[cb] for cb in range(N_LB)], axis=1)
        st_yb_ref[...] = _conv_branch_in(cv, szb_ref[...], gln_ref, bln_ref)
        s_ref[0] = sst_ref[...]
        cs_ref[0] = u_ref[tt - HIST:tt, :]

    @pl.when(jnp.logical_not(live))
    def _():
        for job in sum(back_jobs(), []):
            job()
        y_ref[0] = _rms(y_ref[0], gfin_ref[...])


def _whole():
    return pl.BlockSpec(memory_space=pltpu.VMEM)


def _prompt_call(x, p, w, tt):
    bsz, t_len, _ = x.shape
    seg = tt // SUBLANES
    assert t_len % tt == 0 and tt % CHUNK == 0 and seg >= HIST and seg % SUBLANES == 0
    n_t = t_len // tt
    n_tiles = bsz * n_t
    front = lambda i: jnp.minimum(i, n_tiles - 1)
    back = lambda i: jnp.maximum(i - 1, 0)
    ahead = lambda i: jnp.minimum(i + 1, n_tiles - 1)
    tok = lambda wd, tile: pl.BlockSpec((1, tt, wd), lambda i: (tile(i) // n_t, tile(i) % n_t, 0))
    names = ["gpre", "wa", "wb", "wr", "wup", "bup", "tri", "emat", "wdwb", "bdwb", "ggla", "gln", "bln", "wsq",
             "wpe", "gfin"]
    return pl.pallas_call(
        functools.partial(_prompt_kernel, tt=tt, n_t=n_t, n_tiles=n_tiles),
        grid=(n_tiles + 1,),
        in_specs=[tok(D_MODEL, ahead), tok(D_MODEL, back), tok(PLE, back)] + [_whole()] * len(names),
        out_specs=[tok(D_MODEL, back),
                   pl.BlockSpec((1, HEADS, DK, DV), lambda i: (front(i) // n_t, 0, 0, 0)),
                   pl.BlockSpec((1, HIST, D_MODEL), lambda i: (front(i) // n_t, 0, 0))],
        out_shape=[jax.ShapeDtypeStruct((bsz, t_len, D_MODEL), F32),
                   jax.ShapeDtypeStruct((bsz, HEADS, DK, DV), F32),
                   jax.ShapeDtypeStruct((bsz, HIST, D_MODEL), F32)],
        scratch_shapes=[pltpu.VMEM((2, tt, KEY_DIM), F32), pltpu.VMEM((2, tt, KEY_DIM), F32),
                        pltpu.VMEM((2, tt, KEY_DIM), F32), pltpu.VMEM((2, tt, VAL_DIM), BF16),
                        pltpu.VMEM((2, tt, D_MODEL), BF16),
                        pltpu.VMEM((tt, VAL_DIM), F32),
                        pltpu.VMEM((HEADS * CHUNK, SUB * DK), BF16),
                        pltpu.VMEM((N_LB, (HIST + seg) * SUBLANES, LANES), F32),
                        pltpu.VMEM((N_LB, HIST * SUBLANES, LANES), F32),
                        pltpu.VMEM((N_LB, tt, LANES), F32),
                        pltpu.VMEM((HEADS, DK, DV), F32),
                        pltpu.VMEM((tt, D_MODEL), BF16),
                        pltpu.VMEM((tt, D_MODEL), F32), pltpu.VMEM((tt, D_MODEL), F32),
                        pltpu.VMEM((tt, D_MODEL), F32), pltpu.VMEM((tt, D_MODEL), F32),
                        pltpu.VMEM((tt, D_MODEL), F32), pltpu.VMEM((tt, D_MODEL), BF16),
                        pltpu.VMEM((tt, D_MODEL), F32), pltpu.VMEM((tt, D_MODEL), BF16)],
        compiler_params=pltpu.CompilerParams(dimension_semantics=("arbitrary",),
                                             vmem_limit_bytes=VMEM_LIMIT),
        name="prompt_layer",
    )(x, x, p, *[w[n] for n in names])


def _pre_kernel(x_ref, gpre_ref, wa_ref, wb_ref, wr_ref, wup_ref, bup_ref,
                q_ref, k_ref, la_ref, v_ref, u_ref):
    h = _rms(x_ref[...], gpre_ref[...]).astype(BF16)
    q_ref[...], k_ref[...], la_ref[...], v_ref[...] = _qkv_la(h, wa_ref, wr_ref, wup_ref, bup_ref)
    u_ref[...] = _glu(h, wb_ref)


def _pre_call(x2d, w, tm):
    n = x2d.shape[0]
    row = lambda wd: pl.BlockSpec((tm, wd), lambda i: (i, 0))
    outs = [(KEY_DIM, F32), (KEY_DIM, F32), (KEY_DIM, F32), (VAL_DIM, BF16), (D_MODEL, F32)]
    names = ["gpre", "wa", "wb", "wr", "wup", "bup"]
    return pl.pallas_call(
        _pre_kernel,
        grid=(n // tm,),
        in_specs=[row(D_MODEL)] + [_whole()] * len(names),
        out_specs=[row(wd) for wd, _ in outs],
        out_shape=[jax.ShapeDtypeStruct((n, wd), dt) for wd, dt in outs],
        compiler_params=pltpu.CompilerParams(dimension_semantics=("arbitrary",),
                                             vmem_limit_bytes=VMEM_LIMIT),
        name="sample_pre",
    )(x2d, *[w[n_] for n_ in names])


def _sample_conv_kernel(c0_ref, u_ref, wdwb_ref, bdwb_ref, cv_ref, cs_ref, *, t_len, nbs):
    ext = lambda d, rows, cs: c0_ref[d, rows, cs] if d < HIST else u_ref[d - HIST, rows, cs]
    for r0 in range(0, nbs, SUBLANES):
        rows = slice(r0, r0 + SUBLANES)
        for cb in range(N_LB):
            cs = slice(cb * LANES, (cb + 1) * LANES)
            acc = [bdwb_ref[:, cs] for _ in range(t_len)]
            for d in range(HIST + t_len):
                x = ext(d, rows, cs)
                for t in range(t_len):
                    j = d - t
                    if 0 <= j < CONV_W:
                        acc[t] = acc[t] + wdwb_ref[SUBLANES * j:SUBLANES * (j + 1), cs] * x
            for t in range(t_len):
                cv_ref[t, rows, cs] = acc[t]
    for d in range(HIST):
        cs_ref[d] = ext(d + t_len, slice(None), slice(None))


def _sample_conv_call(c0t, ut, w, nbs):
    _, bsz, _ = c0t.shape
    t_len = ut.shape[0]
    assert bsz % nbs == 0 and nbs % SUBLANES == 0
    blk = lambda rows: pl.BlockSpec((rows, nbs, D_MODEL), lambda i: (0, i, 0))
    return pl.pallas_call(
        functools.partial(_sample_conv_kernel, t_len=t_len, nbs=nbs),
        grid=(bsz // nbs,),
        in_specs=[blk(HIST), blk(t_len), _whole(), _whole()],
        out_specs=[blk(t_len), blk(HIST)],
        out_shape=[jax.ShapeDtypeStruct((t_len, bsz, D_MODEL), F32),
                   jax.ShapeDtypeStruct((HIST, bsz, D_MODEL), F32)],
        compiler_params=pltpu.CompilerParams(dimension_semantics=("arbitrary",),
                                             vmem_limit_bytes=VMEM_LIMIT),
        name="sample_conv",
    )(c0t, ut, w["wdwb"], w["bdwb"])


def _mix_sample_kernel(q_ref, k_ref, la_ref, v_ref, s0_ref, tri_ref, emat_ref,
                       o_ref, s_ref, mcat_ref, *, nb, t_len):
    pad = PACK_ROWS - t_len

    def padf(z):
        zf = z.astype(F32)
        return jnp.concatenate([zf, jnp.zeros((pad, z.shape[1]), F32)], axis=0).astype(z.dtype)

    def pair(m, carry):
        for slot in range(SEQ_UNROLL):
            n = SEQ_UNROLL * m + slot
            s_list = [s0_ref[n, hd] for hd in range(HEADS)]
            o, s_new = _gla_chunk(padf(q_ref[n]), padf(k_ref[n]), padf(la_ref[n]), padf(v_ref[n]),
                                  s_list, tri_ref[...], emat_ref, mcat_ref.at[slot], t_len)
            o_ref[n] = o[0:t_len]
            for hd in range(HEADS):
                s_ref[n, hd] = s_new[hd]
        return carry

    lax.fori_loop(0, nb // SEQ_UNROLL, pair, 0)


def _mix_sample_call(q, k, la, v, s0, w, nb):
    bsz, t_len, _ = q.shape
    assert bsz % nb == 0 and nb % SEQ_UNROLL == 0 and t_len <= SUB
    tok = lambda wd: pl.BlockSpec((nb, t_len, wd), lambda i: (i, 0, 0))
    st = pl.BlockSpec((nb, HEADS, DK, DV), lambda i: (i, 0, 0, 0))
    return pl.pallas_call(
        functools.partial(_mix_sample_kernel, nb=nb, t_len=t_len),
        grid=(bsz // nb,),
        in_specs=[tok(KEY_DIM), tok(KEY_DIM), tok(KEY_DIM), tok(VAL_DIM), st, _whole(), _whole()],
        out_specs=[tok(VAL_DIM), st],
        out_shape=[jax.ShapeDtypeStruct((bsz, t_len, VAL_DIM), F32),
                   jax.ShapeDtypeStruct((bsz, HEADS, DK, DV), F32)],
        scratch_shapes=[pltpu.VMEM((SEQ_UNROLL, HEADS * PACK_ROWS, SUB * DK), BF16)],
        compiler_params=pltpu.CompilerParams(dimension_semantics=("arbitrary",),
                                             vmem_limit_bytes=VMEM_LIMIT),
        name="sample_mix",
    )(q, k, la, v, s0, _tri_const(PACK_ROWS), w["emat"])


def _post_kernel(x_ref, o_ref, cv_ref, p_ref, gpre_ref, wa_ref, wb_ref,
                 ggla_ref, gln_ref, bln_ref, wsq_ref, wpe_ref, gfin_ref, y_ref):
    wao_ref, wbo_ref, wo_ref, wpg_ref = _split_wsq(wsq_ref)
    x = x_ref[...]
    c = D_MODEL
    h = _rms(x, gpre_ref[...]).astype(BF16)
    y_a = _dot(_gated_heads(o_ref[...], _z_a(h, wa_ref), ggla_ref), wao_ref[...])
    szb = _silu(_dot(h, wb_ref[:, 2 * c:3 * c]))
    sga = _sigmoid(_dot(h, wb_ref[:, 3 * c:4 * c]))
    sgb = _sigmoid(_dot(h, wb_ref[:, 4 * c:5 * c]))
    pe = _dot(p_ref[...].astype(BF16), wpe_ref[...])
    y_ref[...] = _tail(x, y_a, cv_ref[...], szb, sga, sgb, pe, gln_ref, bln_ref, wbo_ref, wo_ref, wpg_ref, gfin_ref)


def _post_call(x2d, o, cv, p2d, w, tm):
    n = x2d.shape[0]
    row = lambda wd: pl.BlockSpec((tm, wd), lambda i: (i, 0))
    names = ["gpre", "wa", "wb", "ggla", "gln", "bln", "wsq", "wpe", "gfin"]
    return pl.pallas_call(
        _post_kernel,
        grid=(n // tm,),
        in_specs=[row(D_MODEL)] * 3 + [row(PLE)] + [_whole()] * len(names),
        out_specs=row(D_MODEL),
        out_shape=jax.ShapeDtypeStruct((n, D_MODEL), F32),
        compiler_params=pltpu.CompilerParams(dimension_semantics=("arbitrary",),
                                             vmem_limit_bytes=VMEM_LIMIT),
        name="sample_post",
    )(x2d, o, cv, p2d, *[w[n_] for n_ in names])


def _sample_layer(x, p, s0, c0, w, tm, nb, nbs):
    bsz, t_len, _ = x.shape
    x2d = x.reshape(bsz * t_len, D_MODEL)
    q, k, la, v, u = _pre_call(x2d, w, tm)
    r3 = lambda z: z.reshape(bsz, t_len, z.shape[-1])
    tmaj = lambda z: jnp.transpose(z, (1, 0, 2))
    o, s_new = _mix_sample_call(r3(q), r3(k), r3(la), r3(v), s0, w, nb)
    cv_t, c_new_t = _sample_conv_call(tmaj(c0), tmaj(r3(u)), w, nbs)
    y = _post_call(x2d, o.reshape(-1, VAL_DIM), tmaj(cv_t).reshape(-1, D_MODEL), p.reshape(bsz * t_len, PLE), w, tm)
    return y.reshape(bsz, t_len, D_MODEL), s_new, tmaj(c_new_t)


def kernel(x_prompt, x_sample, state_gla, state_conv, p_prompt, p_sample, g_pre, w_in, w_a_up, b_a_up, g_gla, w_a_out, w_dw, b_dw, g_ln, b_ln, w_b_out, w_o, w_pe, w_pg, g_final):
    depth = w_in.shape[0]
    assert depth == 1, "one trunk layer"
    i = 0
    tl = _tiles()
    n_a = 2 * KEY_DIM + 2 * VAL_DIM
    row = lambda z: z.reshape(1, -1)
    w = {
        "gpre": row(g_pre[i]),
        "wa": w_in[i][:, :n_a].astype(BF16),
        "wb": w_in[i][:, n_a + RANK:].astype(BF16),
        "wr": jnp.pad(w_in[i][:, n_a:n_a + RANK], ((0, 0), (0, LANES - RANK))).astype(BF16),
        "wup": jnp.pad(w_a_up[i], ((0, LANES - RANK), (0, 0))).astype(BF16),
        "bup": row(b_a_up[i]),
        "tri": _tri_const(CHUNK), "emat": _emat_const(),
        "wdwb": jnp.repeat(w_dw[i], SUBLANES, axis=0),
        "bdwb": jnp.broadcast_to(row(b_dw[i]), (SUBLANES, D_MODEL)),
        "ggla": row(g_gla[i]), "gln": row(g_ln[i]), "bln": row(b_ln[i]),
        "wsq": jnp.stack([w_a_out[i], w_b_out[i], w_o[i], w_pg[i]]).astype(BF16),
        "wpe": w_pe[i].astype(BF16), "gfin": row(g_final),
    }
    yp, sp, cp = _prompt_call(x_prompt, p_prompt[i], w, tl["tt"])
    ys, ss, cs = _sample_layer(x_sample, p_sample[i], state_gla[i], state_conv[i], w, tl["tm"], tl["nb"],
                               tl["nbs"])
    return (yp, ys, sp[None], cp[None], ss[None], cs[None])
```

```python
import functools

import numpy as np
import jax
import jax.numpy as jnp
from jax import lax
from jax.experimental import pallas as pl
from jax.experimental.pallas import tpu as pltpu

F32 = jnp.float32
BF16 = jnp.bfloat16

D_MODEL = 1024
HEADS = 4
DK = 128
DV = 256
KEY_DIM = HEADS * DK
VAL_DIM = HEADS * DV
RANK = 16
TAU = 16.0
CHUNK = 64
SUB = 8
PACK_ROWS = 16
CONV_W = 31
HIST = CONV_W - 1
PLE = 256
EPS = 1e-6
LOG2E = 1.4426950408889634
LANES = 128
SUBLANES = 8
N_LB = D_MODEL // LANES
MXU_SLICE = 256
SEQ_UNROLL = 4
VMEM_LIMIT = 62 * 1024 * 1024


def _tiles():
    return dict(tt=256, tm=256, nb=8, nbs=16)


def _dot(a, b):
    return jnp.dot(a, b, preferred_element_type=F32)


def _sigmoid(x):
    return 1.0 / (1.0 + jnp.exp2(x * (-LOG2E)))


def _silu(x):
    return x * _sigmoid(x)


def _log_sigmoid(x):
    return -(jnp.maximum(-x, 0.0) + jnp.log(1.0 + jnp.exp(-jnp.abs(x))))


def _rms(x, g):
    return x * lax.rsqrt(jnp.mean(x * x, axis=-1, keepdims=True) + EPS) * g


def _qkv_la(h, wa_ref, wr_ref, wup_ref, bup_ref):
    q = _dot(h, wa_ref[:, 0:KEY_DIM]) * (DK ** -0.5)
    k = _dot(h, wa_ref[:, KEY_DIM:2 * KEY_DIM])
    v = _dot(h, wa_ref[:, 2 * KEY_DIM:2 * KEY_DIM + VAL_DIM]).astype(BF16)
    r = _dot(h, wr_ref[...]).astype(BF16)
    la = _log_sigmoid(_dot(r, wup_ref[...]) + bup_ref[...]) * (LOG2E / TAU)
    return q, k, la, v


def _z_a(h, wa_ref):
    return _silu(_dot(h, wa_ref[:, 2 * KEY_DIM + VAL_DIM:]))


def _glu(h, wb_ref):
    c = D_MODEL
    return _dot(h, wb_ref[:, 0:c]) * _sigmoid(_dot(h, wb_ref[:, c:2 * c]))


def _gated_heads(o, sza, ggla_ref):
    g = ggla_ref[...]
    on = jnp.concatenate([_rms(o[:, hd * DV:(hd + 1) * DV], g) for hd in range(HEADS)], axis=1)
    return (on * sza).astype(BF16)


def _conv_branch_in(cv, szb, gln_ref, bln_ref):
    mu = jnp.mean(cv, axis=-1, keepdims=True)
    xc = cv - mu
    ln = xc * lax.rsqrt(jnp.mean(xc * xc, axis=-1, keepdims=True) + EPS) * gln_ref[...] + bln_ref[...]
    return (_silu(ln) * szb).astype(BF16)


def _tail(x, y_a, cv, szb, sga, sgb, pe, gln_ref, bln_ref, wbo_ref, wo_ref, wpg_ref, gfin_ref):
    y_b = _dot(_conv_branch_in(cv, szb, gln_ref, bln_ref), wbo_ref[...])
    merged = sga * y_a + sgb * y_b
    x1 = x + _dot(merged.astype(BF16), wo_ref[...])
    x2 = x1 + _sigmoid(_dot(x1.astype(BF16), wpg_ref[...])) * pe
    return _rms(x2, gfin_ref[...])


def _gla_chunk(q, k, a, v, s_list, tri, emat, mcat_ref, n_keys, tick=lambda: None):
    c = q.shape[0]
    n_sub = c // SUB
    k_used = min(SUB, n_keys)
    a0 = a.astype(BF16)
    r1 = a - a0.astype(F32)
    a1 = r1.astype(BF16)
    a2 = (r1 - a1.astype(F32)).astype(BF16)
    b = _dot(tri, a0) + _dot(tri, a1) + _dot(tri, a2)

    att_off, q_dec, k_dec, d_last = [], [], [], []
    for h in range(HEADS):
        sl = slice(h * DK, (h + 1) * DK)
        qh, kh, bh = q[:, sl], k[:, sl], b[:, sl]
        bl = bh[c - 1:c, :]
        q_dec.append((qh * jnp.exp2(bh)).astype(BF16))
        k_dec.append(kh * jnp.exp2(bl - bh))
        d_last.append(jnp.exp2(bl))
        zrow = jnp.zeros((SUB, DK), F32)
        rows = [jnp.zeros((SUB, c), F32)]
        for i in range(1, n_sub):
            lo = i * SUB
            if lo >= n_keys:
                rows.append(jnp.zeros((SUB, c), F32))
                continue
            br = bh[lo - 1:lo, :]
            qi = jnp.concatenate([qh[lo:lo + SUB] * jnp.exp2(bh[lo:lo + SUB] - br), zrow], axis=0).astype(BF16)
            ki = kh[:lo] * jnp.exp2(br - bh[:lo])
            kfull = jnp.concatenate([ki, jnp.zeros((c - lo, DK), F32)], axis=0).astype(BF16)
            rows.append(lax.dot_general(qi, kfull, (((1,), (1,)), ((), ())),
                                        preferred_element_type=F32)[0:SUB])
        att_off.append(jnp.concatenate(rows, axis=0))
        for g0 in range(0, c, PACK_ROWS):
            for s in range(k_used):
                halves = []
                for lo in (g0, g0 + SUB):
                    if lo + s < n_keys:
                        qb, bb = qh[lo:lo + SUB], bh[lo:lo + SUB]
                        dec = jnp.exp2(jnp.minimum(bb - bh[lo + s:lo + s + 1, :], 0.0))
                        halves.append(qb * dec * kh[lo + s:lo + s + 1, :])
                    else:
                        halves.append(zrow)
                mcat_ref[h * c + g0:h * c + g0 + PACK_ROWS, s * DK:(s + 1) * DK] = (
                    jnp.concatenate(halves, axis=0).astype(BF16))
        tick()

    rsum = _dot(mcat_ref[:, 0:k_used * DK], emat[0:k_used * DK, :])
    ti = lax.broadcasted_iota(jnp.int32, (c, c), 0)
    si = lax.broadcasted_iota(jnp.int32, (c, c), 1)
    diag_mask = (ti // SUB == si // SUB) & (si <= ti)

    o_parts, s_new = [], []
    for h in range(HEADS):
        vh = v[:, h * DV:(h + 1) * DV]
        att = att_off[h] + jnp.where(diag_mask, rsum[h * c:(h + 1) * c, 0:c], 0.0)
        lhs = jnp.concatenate([q_dec[h], att.astype(BF16)], axis=1)
        rhs = jnp.concatenate([s_list[h].astype(BF16), vh], axis=0)
        o_parts.append(_dot(lhs, rhs))
        dcol = jnp.broadcast_to(d_last[h], (DK, DK)).T
        kt = k_dec[h].T.astype(BF16)
        s_new.append(jnp.concatenate([dcol, dcol], axis=1) * s_list[h] + _dot(kt, vh))
    return jnp.concatenate(o_parts, axis=1), s_new


def _tri_const(c):
    return jnp.asarray(np.tril(np.ones((c, c), np.float32)), BF16)


def _emat_const():
    e = np.zeros((SUB * DK, LANES), np.float32)
    rows = np.arange(SUB * DK)
    for l in range(LANES):
        e[rows[rows // DK == l % SUB], l] = 1.0
    return jnp.asarray(e, BF16)


def _conv_tile(u_ref, xs_ref, hist_ref, cvs_ref, wdwb_ref, bdwb_ref, tt, tick=lambda: None):
    seg = tt // SUBLANES
    x0 = HIST * SUBLANES
    sub7 = lax.broadcasted_iota(jnp.int32, (SUBLANES, LANES), 0) == SUBLANES - 1
    grp = SUBLANES
    for cb in range(N_LB):
        cs = slice(cb * LANES, (cb + 1) * LANES)
        for s in range(SUBLANES):
            for a0 in range(0, seg, SUBLANES):
                val = u_ref[seg * s + a0:seg * s + a0 + SUBLANES, cs]
                xs_ref[cb, pl.ds(x0 + SUBLANES * a0 + s, SUBLANES, stride=SUBLANES), :] = val
        for a in range(seg - HIST, seg):
            e = a - (seg - HIST)
            new = xs_ref[cb, x0 + SUBLANES * a:x0 + SUBLANES * (a + 1), :]
            old = hist_ref[cb, SUBLANES * e:SUBLANES * (e + 1), :]
            xs_ref[cb, SUBLANES * e:SUBLANES * (e + 1), :] = pltpu.roll(jnp.where(sub7, old, new), 1, axis=0)
            hist_ref[cb, SUBLANES * e:SUBLANES * (e + 1), :] = new
        w = [wdwb_ref[SUBLANES * j:SUBLANES * (j + 1), cs] for j in range(CONV_W)]
        for g0 in range(0, seg, grp):
            if g0 % (seg // 2) == 0:
                tick()
            acc = [bdwb_ref[:, cs] for _ in range(grp)]
            for d in range(g0, g0 + grp + HIST):
                x = xs_ref[cb, SUBLANES * d:SUBLANES * (d + 1), :]
                for i in range(grp):
                    j = d - (g0 + i)
                    if 0 <= j < CONV_W:
                        acc[i] = acc[i] + w[j] * x
            for i in range(grp):
                cvs_ref[cb, pl.ds(g0 + i, SUBLANES, stride=seg), :] = acc[i]


def _split_wsq(wsq_ref):
    return wsq_ref.at[0], wsq_ref.at[1], wsq_ref.at[2], wsq_ref.at[3]


def _prompt_kernel(xn_ref, xb_ref, pb_ref, gpre_ref, wa_ref, wb_ref, wr_ref, wup_ref, bup_ref, tri_ref, emat_ref,
                   wdwb_ref, bdwb_ref, ggla_ref, gln_ref, bln_ref, wsq_ref, wpe_ref,
                   gfin_ref,
                   y_ref, s_ref, cs_ref,
                   q_ref, k_ref, la_ref, v_ref, h_ref, o_ref, mcat_ref, xs_ref, hist_ref, cvs_ref, sst_ref,
                   st_yb_ref, st_ma_ref, st_sgb_ref, u_ref, sza_ref, szb_ref, mg_ref, x1_ref, x1b_ref,
                   *, tt, n_t, n_tiles):
    wao_ref, wbo_ref, wo_ref, wpg_ref = _split_wsq(wsq_ref)
    i = pl.program_id(0)
    live = i < n_tiles
    t = jnp.minimum(i, n_tiles - 1) % n_t
    cur = i % 2
    nxt = 1 - cur

    @pl.when(jnp.logical_and(t == 0, live))
    def _():
        sst_ref[...] = jnp.zeros(sst_ref.shape, F32)
        hist_ref[...] = jnp.zeros(hist_ref.shape, F32)

    @pl.when(i == 0)
    def _():
        st_yb_ref[...] = jnp.zeros(st_yb_ref.shape, BF16)
        st_ma_ref[...] = jnp.zeros(st_ma_ref.shape, F32)
        st_sgb_ref[...] = jnp.zeros(st_sgb_ref.shape, F32)
        h0 = _rms(xb_ref[0], gpre_ref[...]).astype(BF16)
        h_ref[0] = h0
        q_ref[0], k_ref[0], la_ref[0], v_ref[0] = _qkv_la(h0, wa_ref, wr_ref, wup_ref, bup_ref)

    c = D_MODEL
    za0 = 2 * KEY_DIM + VAL_DIM

    def sliced(store, fn, width=D_MODEL):
        return [functools.partial(lambda lo: store(lo, lo + MXU_SLICE, fn(lo, lo + MXU_SLICE)), n * MXU_SLICE)
                for n in range(width // MXU_SLICE)]

    def into(ref, *lead):
        def store(lo, hi, val):
            ref[(*lead, slice(None), slice(lo, hi))] = val
        return store

    def ticker(jobs, n_ticks):
        state = [0, 0]

        def tick():
            state[0] += 1
            upto = -(-len(jobs) * state[0] // n_ticks)
            while state[1] < min(upto, len(jobs)):
                jobs[state[1]]()
                state[1] += 1
        return tick

    def store_x1(lo, hi, val):
        x1_ref[:, lo:hi] = val
        x1b_ref[:, lo:hi] = val.astype(BF16)

    def back_jobs():
        pb = pb_ref[0].astype(BF16)
        back1 = sliced(into(mg_ref), lambda lo, hi: (
            st_ma_ref[:, lo:hi] + st_sgb_ref[:, lo:hi] * _dot(st_yb_ref[...], wbo_ref[:, lo:hi])).astype(BF16))
        back2 = sliced(store_x1, lambda lo, hi: xb_ref[0, :, lo:hi] + _dot(mg_ref[...], wo_ref[:, lo:hi]))
        back3 = sliced(into(y_ref, 0), lambda lo, hi: (
            x1_ref[:, lo:hi] + _sigmoid(_dot(x1b_ref[...], wpg_ref[:, lo:hi])) * _dot(pb, wpe_ref[:, lo:hi])))
        return back1, back2, back3

    @pl.when(live)
    def _():
        h = h_ref[cur]
        hn = _rms(xn_ref[0], gpre_ref[...]).astype(BF16)
        h_ref[nxt] = hn

        def store_la(lo, hi, r):
            la_ref[nxt] = _log_sigmoid(_dot(r.astype(BF16), wup_ref[...]) + bup_ref[...]) * (LOG2E / TAU)

        ahead = (sliced(into(q_ref, nxt), lambda lo, hi: _dot(hn, wa_ref[:, lo:hi]) * (DK ** -0.5), KEY_DIM)
                 + sliced(into(k_ref, nxt), lambda lo, hi: _dot(hn, wa_ref[:, KEY_DIM + lo:KEY_DIM + hi]),
                          KEY_DIM)
                 + sliced(into(v_ref, nxt), lambda lo, hi: _dot(
                     hn, wa_ref[:, 2 * KEY_DIM + lo:2 * KEY_DIM + hi]).astype(BF16), VAL_DIM)
                 + sliced(store_la, lambda lo, hi: _dot(hn, wr_ref[...]), MXU_SLICE))

        back1, back2, back3 = back_jobs()
        n_chunks = tt // CHUNK
        jobs = (sliced(into(u_ref), lambda lo, hi: (_dot(h, wb_ref[:, lo:hi])
                                                    * _sigmoid(_dot(h, wb_ref[:, c + lo:c + hi]))))
                + back1 + sliced(into(sza_ref), lambda lo, hi: _silu(_dot(h, wa_ref[:, za0 + lo:za0 + hi])))
                + back2 + ahead)
        tick = ticker(jobs, n_chunks * HEADS)
        for ci in range(n_chunks):
            rows = slice(ci * CHUNK, (ci + 1) * CHUNK)
            s_list = [sst_ref[hd] for hd in range(HEADS)]
            o, s_new = _gla_chunk(q_ref[cur, rows, :], k_ref[cur, rows, :], la_ref[cur, rows, :],
                                  v_ref[cur, rows, :], s_list, tri_ref[...], emat_ref, mcat_ref, CHUNK, tick)
            o_ref[rows, :] = o
            for hd in range(HEADS):
                sst_ref[hd] = s_new[hd]

        ona = _gated_heads(o_ref[...], sza_ref[...], ggla_ref)
        jobs = (sliced(into(st_ma_ref), lambda lo, hi: _sigmoid(_dot(h, wb_ref[:, 3 * c + lo:3 * c + hi])))
                + back3
                + sliced(into(st_ma_ref), lambda lo, hi: st_ma_ref[:, lo:hi] * _dot(ona, wao_ref[:, lo:hi]))
                + sliced(into(szb_ref), lambda lo, hi: _silu(_dot(h, wb_ref[:, 2 * c + lo:2 * c + hi])))
                + sliced(into(st_sgb_ref), lambda lo, hi: _sigmoid(_dot(h, wb_ref[:, 4 * c + lo:4 * c + hi]))))
        tick = ticker(jobs, 2 * N_LB)
        _conv_tile(u_ref, xs_ref, hist_ref, cvs_ref, wdwb_ref, bdwb_ref, tt, tick)
        y_ref[0] = _rms(y_ref[0], gfin_ref[...])

        cv = jnp.concatenate([cvs_ref[cb] for cb in range(N_LB)], axis=1)
        st_yb_ref[...] = _conv_branch_in(cv, szb_ref[...], gln_ref, bln_ref)
        s_ref[0] = sst_ref[...]
        cs_ref[0] = u_ref[tt - HIST:tt, :]

    @pl.when(jnp.logical_not(live))
    def _():
        for job in sum(back_jobs(), []):
            job()
        y_ref[0] = _rms(y_ref[0], gfin_ref[...])


def _whole():
    return pl.BlockSpec(memory_space=pltpu.VMEM)


def _prompt_call(x, p, w, tt):
    bsz, t_len, _ = x.shape
    seg = tt // SUBLANES
    assert t_len % tt == 0 and tt % CHUNK == 0 and seg >= HIST and seg % SUBLANES == 0
    n_t = t_len // tt
    n_tiles = bsz * n_t
    front = lambda i: jnp.minimum(i, n_tiles - 1)
    back = lambda i: jnp.maximum(i - 1, 0)
    ahead = lambda i: jnp.minimum(i + 1, n_tiles - 1)
    tok = lambda wd, tile: pl.BlockSpec((1, tt, wd), lambda i: (tile(i) // n_t, tile(i) % n_t, 0))
    names = ["gpre", "wa", "wb", "wr", "wup", "bup", "tri", "emat", "wdwb", "bdwb", "ggla", "gln", "bln", "wsq",
             "wpe", "gfin"]
    return pl.pallas_call(
        functools.partial(_prompt_kernel, tt=tt, n_t=n_t, n_tiles=n_tiles),
        grid=(n_tiles + 1,),
        in_specs=[tok(D_MODEL, ahead), tok(D_MODEL, back), tok(PLE, back)] + [_whole()] * len(names),
        out_specs=[tok(D_MODEL, back),
                   pl.BlockSpec((1, HEADS, DK, DV), lambda i: (front(i) // n_t, 0, 0, 0)),
                   pl.BlockSpec((1, HIST, D_MODEL), lambda i: (front(i) // n_t, 0, 0))],
        out_shape=[jax.ShapeDtypeStruct((bsz, t_len, D_MODEL), F32),
                   jax.ShapeDtypeStruct((bsz, HEADS, DK, DV), F32),
                   jax.ShapeDtypeStruct((bsz, HIST, D_MODEL), F32)],
        scratch_shapes=[pltpu.VMEM((2, tt, KEY_DIM), F32), pltpu.VMEM((2, tt, KEY_DIM), F32),
                        pltpu.VMEM((2, tt, KEY_DIM), F32), pltpu.VMEM((2, tt, VAL_DIM), BF16),
                        pltpu.VMEM((2, tt, D_MODEL), BF16),
                        pltpu.VMEM((tt, VAL_DIM), F32),
                        pltpu.VMEM((HEADS * CHUNK, SUB * DK), BF16),
                        pltpu.VMEM((N_LB, (HIST + seg) * SUBLANES, LANES), F32),
                        pltpu.VMEM((N_LB, HIST * SUBLANES, LANES), F32),
                        pltpu.VMEM((N_LB, tt, LANES), F32),
                        pltpu.VMEM((HEADS, DK, DV), F32),
                        pltpu.VMEM((tt, D_MODEL), BF16),
                        pltpu.VMEM((tt, D_MODEL), F32), pltpu.VMEM((tt, D_MODEL), F32),
                        pltpu.VMEM((tt, D_MODEL), F32), pltpu.VMEM((tt, D_MODEL), F32),
                        pltpu.VMEM((tt, D_MODEL), F32), pltpu.VMEM((tt, D_MODEL), BF16),
                        pltpu.VMEM((tt, D_MODEL), F32), pltpu.VMEM((tt, D_MODEL), BF16)],
        compiler_params=pltpu.CompilerParams(dimension_semantics=("arbitrary",),
                                             vmem_limit_bytes=VMEM_LIMIT),
        name="prompt_layer",
    )(x, x, p, *[w[n] for n in names])


def _pre_kernel(x_ref, gpre_ref, wa_ref, wb_ref, wr_ref, wup_ref, bup_ref,
                q_ref, k_ref, la_ref, v_ref, u_ref):
    h = _rms(x_ref[...], gpre_ref[...]).astype(BF16)
    q_ref[...], k_ref[...], la_ref[...], v_ref[...] = _qkv_la(h, wa_ref, wr_ref, wup_ref, bup_ref)
    u_ref[...] = _glu(h, wb_ref)


def _pre_call(x2d, w, tm):
    n = x2d.shape[0]
    row = lambda wd: pl.BlockSpec((tm, wd), lambda i: (i, 0))
    outs = [(KEY_DIM, F32), (KEY_DIM, F32), (KEY_DIM, F32), (VAL_DIM, BF16), (D_MODEL, F32)]
    names = ["gpre", "wa", "wb", "wr", "wup", "bup"]
    return pl.pallas_call(
        _pre_kernel,
        grid=(n // tm,),
        in_specs=[row(D_MODEL)] + [_whole()] * len(names),
        out_specs=[row(wd) for wd, _ in outs],
        out_shape=[jax.ShapeDtypeStruct((n, wd), dt) for wd, dt in outs],
        compiler_params=pltpu.CompilerParams(dimension_semantics=("arbitrary",),
                                             vmem_limit_bytes=VMEM_LIMIT),
        name="sample_pre",
    )(x2d, *[w[n_] for n_ in names])


def _sample_conv_kernel(c0_ref, u_ref, wdwb_ref, bdwb_ref, cv_ref, cs_ref, *, t_len, nbs):
    ext = lambda d, rows, cs: c0_ref[d, rows, cs] if d < HIST else u_ref[d - HIST, rows, cs]
    for r0 in range(0, nbs, SUBLANES):
        rows = slice(r0, r0 + SUBLANES)
        for cb in range(N_LB):
            cs = slice(cb * LANES, (cb + 1) * LANES)
            acc = [bdwb_ref[:, cs] for _ in range(t_len)]
            for d in range(HIST + t_len):
                x = ext(d, rows, cs)
                for t in range(t_len):
                    j = d - t
                    if 0 <= j < CONV_W:
                        acc[t] = acc[t] + wdwb_ref[SUBLANES * j:SUBLANES * (j + 1), cs] * x
            for t in range(t_len):
                cv_ref[t, rows, cs] = acc[t]
    for d in range(HIST):
        cs_ref[d] = ext(d + t_len, slice(None), slice(None))


def _sample_conv_call(c0t, ut, w, nbs):
    _, bsz, _ = c0t.shape
    t_len = ut.shape[0]
    assert bsz % nbs == 0 and nbs % SUBLANES == 0
    blk = lambda rows: pl.BlockSpec((rows, nbs, D_MODEL), lambda i: (0, i, 0))
    return pl.pallas_call(
        functools.partial(_sample_conv_kernel, t_len=t_len, nbs=nbs),
        grid=(bsz // nbs,),
        in_specs=[blk(HIST), blk(t_len), _whole(), _whole()],
        out_specs=[blk(t_len), blk(HIST)],
        out_shape=[jax.ShapeDtypeStruct((t_len, bsz, D_MODEL), F32),
                   jax.ShapeDtypeStruct((HIST, bsz, D_MODEL), F32)],
        compiler_params=pltpu.CompilerParams(dimension_semantics=("arbitrary",),
                                             vmem_limit_bytes=VMEM_LIMIT),
        name="sample_conv",
    )(c0t, ut, w["wdwb"], w["bdwb"])


def _mix_sample_kernel(q_ref, k_ref, la_ref, v_ref, s0_ref, tri_ref, emat_ref,
                       o_ref, s_ref, mcat_ref, *, nb, t_len):
    pad = PACK_ROWS - t_len

    def padf(z):
        zf = z.astype(F32)
        return jnp.concatenate([zf, jnp.zeros((pad, z.shape[1]), F32)], axis=0).astype(z.dtype)

    def pair(m, carry):
        for slot in range(SEQ_UNROLL):
            n = SEQ_UNROLL * m + slot
            s_list = [s0_ref[n, hd] for hd in range(HEADS)]
            o, s_new = _gla_chunk(padf(q_ref[n]), padf(k_ref[n]), padf(la_ref[n]), padf(v_ref[n]),
                                  s_list, tri_ref[...], emat_ref, mcat_ref.at[slot], t_len)
            o_ref[n] = o[0:t_len]
            for hd in range(HEADS):
                s_ref[n, hd] = s_new[hd]
        return carry

    lax.fori_loop(0, nb // SEQ_UNROLL, pair, 0)


def _mix_sample_call(q, k, la, v, s0, w, nb):
    bsz, t_len, _ = q.shape
    assert bsz % nb == 0 and nb % SEQ_UNROLL == 0 and t_len <= SUB
    tok = lambda wd: pl.BlockSpec((nb, t_len, wd), lambda i: (i, 0, 0))
    st = pl.BlockSpec((nb, HEADS, DK, DV), lambda i: (i, 0, 0, 0))
    return pl.pallas_call(
        functools.partial(_mix_sample_kernel, nb=nb, t_len=t_len),
        grid=(bsz // nb,),
        in_specs=[tok(KEY_DIM), tok(KEY_DIM), tok(KEY_DIM), tok(VAL_DIM), st, _whole(), _whole()],
        out_specs=[tok(VAL_DIM), st],
        out_shape=[jax.ShapeDtypeStruct((bsz, t_len, VAL_DIM), F32),
                   jax.ShapeDtypeStruct((bsz, HEADS, DK, DV), F32)],
        scratch_shapes=[pltpu.VMEM((SEQ_UNROLL, HEADS * PACK_ROWS, SUB * DK), BF16)],
        compiler_params=pltpu.CompilerParams(dimension_semantics=("arbitrary",),
                                             vmem_limit_bytes=VMEM_LIMIT),
        name="sample_mix",
    )(q, k, la, v, s0, _tri_const(PACK_ROWS), w["emat"])


def _post_kernel(x_ref, o_ref, cv_ref, p_ref, gpre_ref, wa_ref, wb_ref,
                 ggla_ref, gln_ref, bln_ref, wsq_ref, wpe_ref, gfin_ref, y_ref):
    wao_ref, wbo_ref, wo_ref, wpg_ref = _split_wsq(wsq_ref)
    x = x_ref[...]
    c = D_MODEL
    h = _rms(x, gpre_ref[...]).astype(BF16)
    y_a = _dot(_gated_heads(o_ref[...], _z_a(h, wa_ref), ggla_ref), wao_ref[...])
    szb = _silu(_dot(h, wb_ref[:, 2 * c:3 * c]))
    sga = _sigmoid(_dot(h, wb_ref[:, 3 * c:4 * c]))
    sgb = _sigmoid(_dot(h, wb_ref[:, 4 * c:5 * c]))
    pe = _dot(p_ref[...].astype(BF16), wpe_ref[...])
    y_ref[...] = _tail(x, y_a, cv_ref[...], szb, sga, sgb, pe, gln_ref, bln_ref, wbo_ref, wo_ref, wpg_ref, gfin_ref)


def _post_call(x2d, o, cv, p2d, w, tm):
    n = x2d.shape[0]
    row = lambda wd: pl.BlockSpec((tm, wd), lambda i: (i, 0))
    names = ["gpre", "wa", "wb", "ggla", "gln", "bln", "wsq", "wpe", "gfin"]
    return pl.pallas_call(
        _post_kernel,
        grid=(n // tm,),
        in_specs=[row(D_MODEL)] * 3 + [row(PLE)] + [_whole()] * len(names),
        out_specs=row(D_MODEL),
        out_shape=jax.ShapeDtypeStruct((n, D_MODEL), F32),
        compiler_params=pltpu.CompilerParams(dimension_semantics=("arbitrary",),
                                             vmem_limit_bytes=VMEM_LIMIT),
        name="sample_post",
    )(x2d, o, cv, p2d, *[w[n_] for n_ in names])


def _sample_layer(x, p, s0, c0, w, tm, nb, nbs):
    bsz, t_len, _ = x.shape
    x2d = x.reshape(bsz * t_len, D_MODEL)
    q, k, la, v, u = _pre_call(x2d, w, tm)
    r3 = lambda z: z.reshape(bsz, t_len, z.shape[-1])
    tmaj = lambda z: jnp.transpose(z, (1, 0, 2))
    o, s_new = _mix_sample_call(r3(q), r3(k), r3(la), r3(v), s0, w, nb)
    cv_t, c_new_t = _sample_conv_call(tmaj(c0), tmaj(r3(u)), w, nbs)
    y = _post_call(x2d, o.reshape(-1, VAL_DIM), tmaj(cv_t).reshape(-1, D_MODEL), p.reshape(bsz * t_len, PLE), w, tm)
    return y.reshape(bsz, t_len, D_MODEL), s_new, tmaj(c_new_t)


PREP_ROWS = 128
SHIFT_K = MXU_SLICE + LANES


def _shift_consts():
    p = np.zeros((SHIFT_K, MXU_SLICE), np.float32)
    p[np.arange(MXU_SLICE) + RANK, np.arange(MXU_SLICE)] = 1.0
    pt = np.zeros((LANES, MXU_SLICE), np.float32)
    pt[np.arange(RANK), MXU_SLICE - RANK + np.arange(RANK)] = 1.0
    return jnp.asarray(p, BF16), jnp.asarray(pt, BF16)


def _prep_kernel(w_ref, tail_ref, p_ref, pt_ref, wa_ref, wb_ref, wr_ref):
    n_a = 2 * KEY_DIM + 2 * VAL_DIM
    n_sl = 5 * D_MODEL // MXU_SLICE
    wa_ref[...] = w_ref[0, :, 0:n_a].astype(BF16)
    lane = lax.broadcasted_iota(jnp.int32, (PREP_ROWS, LANES), 1)
    wr_ref[...] = jnp.where(lane < RANK, w_ref[0, :, n_a:n_a + LANES], 0.0).astype(BF16)
    for n in range(n_sl):
        lo = n_a + n * MXU_SLICE
        if n < n_sl - 1:
            out = _dot(w_ref[0, :, lo:lo + SHIFT_K].astype(BF16), p_ref[...])
        else:
            out = (_dot(w_ref[0, :, lo:lo + MXU_SLICE].astype(BF16), p_ref[0:MXU_SLICE, :])
                   + _dot(tail_ref[...].astype(BF16), pt_ref[...]))
        wb_ref[:, n * MXU_SLICE:(n + 1) * MXU_SLICE] = out.astype(BF16)


def _prep_call(w_in):
    _, d, d_in = w_in.shape
    n_a = 2 * KEY_DIM + 2 * VAL_DIM
    n_b = 5 * D_MODEL
    assert d_in == n_a + RANK + n_b and d % PREP_ROWS == 0
    tail = jnp.pad(w_in[0][:, d_in - RANK:], ((0, 0), (0, LANES - RANK)))
    p, pt = _shift_consts()
    rows = lambda wd: pl.BlockSpec((PREP_ROWS, wd), lambda r: (r, 0))
    return pl.pallas_call(
        _prep_kernel,
        grid=(d // PREP_ROWS,),
        in_specs=[pl.BlockSpec((1, PREP_ROWS, d_in), lambda r: (0, r, 0)), rows(LANES), _whole(), _whole()],
        out_specs=[rows(n_a), rows(n_b), rows(LANES)],
        out_shape=[jax.ShapeDtypeStruct((d, n_a), BF16), jax.ShapeDtypeStruct((d, n_b), BF16),
                   jax.ShapeDtypeStruct((d, LANES), BF16)],
        compiler_params=pltpu.CompilerParams(dimension_semantics=("arbitrary",),
                                             vmem_limit_bytes=VMEM_LIMIT),
        name="weight_prep",
    )(w_in, tail, p, pt)


def kernel(x_prompt, x_sample, state_gla, state_conv, p_prompt, p_sample, g_pre, w_in, w_a_up, b_a_up, g_gla, w_a_out, w_dw, b_dw, g_ln, b_ln, w_b_out, w_o, w_pe, w_pg, g_final):
    depth = w_in.shape[0]
    assert depth == 1, "one trunk layer"
    i = 0
    tl = _tiles()
    row = lambda z: z.reshape(1, -1)
    wa, wb, wr = _prep_call(w_in)
    w = {
        "gpre": row(g_pre[i]),
        "wa": wa, "wb": wb, "wr": wr,
        "wup": jnp.pad(w_a_up[i], ((0, LANES - RANK), (0, 0))).astype(BF16),
        "bup": row(b_a_up[i]),
        "tri": _tri_const(CHUNK), "emat": _emat_const(),
        "wdwb": jnp.repeat(w_dw[i], SUBLANES, axis=0),
        "bdwb": jnp.broadcast_to(row(b_dw[i]), (SUBLANES, D_MODEL)),
        "ggla": row(g_gla[i]), "gln": row(g_ln[i]), "bln": row(b_ln[i]),
        "wsq": jnp.stack([w_a_out[i], w_b_out[i], w_o[i], w_pg[i]]).astype(BF16),
        "wpe": w_pe[i].astype(BF16), "gfin": row(g_final),
    }
    yp, sp, cp = _prompt_call(x_prompt, p_prompt[i], w, tl["tt"])
    ys, ss, cs = _sample_layer(x_sample, p_sample[i], state_gla[i], state_conv[i], w, tl["tm"], tl["nb"],
                               tl["nbs"])
    return (yp, ys, sp[None], cp[None], ss[None], cs[None])
```

```python
import functools

import numpy as np
import jax
import jax.numpy as jnp
from jax import lax
from jax.experimental import pallas as pl
from jax.experimental.pallas import tpu as pltpu

F32 = jnp.float32
BF16 = jnp.bfloat16

D_MODEL = 1024
HEADS = 4
DK = 128
DV = 256
KEY_DIM = HEADS * DK
VAL_DIM = HEADS * DV
RANK = 16
TAU = 16.0
CHUNK = 64
SUB = 8
PACK_ROWS = 16
CONV_W = 31
HIST = CONV_W - 1
PLE = 256
EPS = 1e-6
LOG2E = 1.4426950408889634
LANES = 128
SUBLANES = 8
N_LB = D_MODEL // LANES
MXU_SLICE = 256
SEQ_UNROLL = 4
VMEM_LIMIT = 62 * 1024 * 1024


def _tiles():
    return dict(tt=256, tm=256, nb=8, nbs=16)


def _dot(a, b):
    return jnp.dot(a, b, preferred_element_type=F32)


def _sigmoid(x):
    return 1.0 / (1.0 + jnp.exp2(x * (-LOG2E)))


def _silu(x):
    return x * _sigmoid(x)


def _log_sigmoid(x):
    return -(jnp.maximum(-x, 0.0) + jnp.log(1.0 + jnp.exp(-jnp.abs(x))))


def _rms(x, g):
    return x * lax.rsqrt(jnp.mean(x * x, axis=-1, keepdims=True) + EPS) * g


def _qkv_la(h, wa_ref, wr_ref, wup_ref, bup_ref):
    q = _dot(h, wa_ref[:, 0:KEY_DIM]) * (DK ** -0.5)
    k = _dot(h, wa_ref[:, KEY_DIM:2 * KEY_DIM])
    v = _dot(h, wa_ref[:, 2 * KEY_DIM:2 * KEY_DIM + VAL_DIM]).astype(BF16)
    r = _dot(h, wr_ref[...]).astype(BF16)
    la = _log_sigmoid(_dot(r, wup_ref[...]) + bup_ref[...]) * (LOG2E / TAU)
    return q, k, la, v


def _z_a(h, wa_ref):
    return _silu(_dot(h, wa_ref[:, 2 * KEY_DIM + VAL_DIM:]))


def _glu(h, wb_ref):
    c = D_MODEL
    return _dot(h, wb_ref[:, 0:c]) * _sigmoid(_dot(h, wb_ref[:, c:2 * c]))


def _gated_heads(o, sza, ggla_ref):
    g = ggla_ref[...]
    on = jnp.concatenate([_rms(o[:, hd * DV:(hd + 1) * DV], g) for hd in range(HEADS)], axis=1)
    return (on * sza).astype(BF16)


def _conv_branch_in(cv, szb, gln_ref, bln_ref):
    mu = jnp.mean(cv, axis=-1, keepdims=True)
    xc = cv - mu
    ln = xc * lax.rsqrt(jnp.mean(xc * xc, axis=-1, keepdims=True) + EPS) * gln_ref[...] + bln_ref[...]
    return (_silu(ln) * szb).astype(BF16)


def _tail(x, y_a, cv, szb, sga, sgb, pe, gln_ref, bln_ref, wbo_ref, wo_ref, wpg_ref, gfin_ref):
    y_b = _dot(_conv_branch_in(cv, szb, gln_ref, bln_ref), wbo_ref[...])
    merged = sga * y_a + sgb * y_b
    x1 = x + _dot(merged.astype(BF16), wo_ref[...])
    x2 = x1 + _sigmoid(_dot(x1.astype(BF16), wpg_ref[...])) * pe
    return _rms(x2, gfin_ref[...])


def _gla_chunk(q, k, a, v, s_list, tri, n_keys, tick=lambda: None):
    c = q.shape[0]
    n_sub = c // SUB
    k_used = min(SUB, n_keys)
    a0 = a.astype(BF16)
    r1 = a - a0.astype(F32)
    a1 = r1.astype(BF16)
    a2 = (r1 - a1.astype(F32)).astype(BF16)
    b = _dot(tri, a0) + _dot(tri, a1) + _dot(tri, a2)

    att_off, att_diag, q_dec, k_dec, d_last = [], [], [], [], []
    for h in range(HEADS):
        sl = slice(h * DK, (h + 1) * DK)
        qh, kh, bh = q[:, sl], k[:, sl], b[:, sl]
        bl = bh[c - 1:c, :]
        q_dec.append((qh * jnp.exp2(bh)).astype(BF16))
        k_dec.append(kh * jnp.exp2(bl - bh))
        d_last.append(jnp.exp2(bl))
        zrow = jnp.zeros((SUB, DK), F32)
        rows = [jnp.zeros((SUB, c), F32)]
        for i in range(1, n_sub):
            lo = i * SUB
            if lo >= n_keys:
                rows.append(jnp.zeros((SUB, c), F32))
                continue
            br = bh[lo - 1:lo, :]
            qi = jnp.concatenate([qh[lo:lo + SUB] * jnp.exp2(bh[lo:lo + SUB] - br), zrow], axis=0).astype(BF16)
            ki = kh[:lo] * jnp.exp2(br - bh[:lo])
            kfull = jnp.concatenate([ki, jnp.zeros((c - lo, DK), F32)], axis=0).astype(BF16)
            rows.append(lax.dot_general(qi, kfull, (((1,), (1,)), ((), ())),
                                        preferred_element_type=F32)[0:SUB])
        att_off.append(jnp.concatenate(rows, axis=0))
        col = lax.broadcasted_iota(jnp.int32, (SUB, c), 1)
        blocks = []
        for lo in range(0, c, SUB):
            blk = jnp.zeros((SUB, c), F32)
            for s in range(k_used):
                if lo + s < n_keys:
                    qb, bb = qh[lo:lo + SUB], bh[lo:lo + SUB]
                    dec = jnp.exp2(jnp.minimum(bb - bh[lo + s:lo + s + 1, :], 0.0))
                    m = qb * dec * kh[lo + s:lo + s + 1, :]
                    blk = jnp.where(col == lo + s, jnp.sum(m, axis=-1, keepdims=True), blk)
            blocks.append(blk)
        att_diag.append(jnp.concatenate(blocks, axis=0))
        tick()

    ti = lax.broadcasted_iota(jnp.int32, (c, c), 0)
    si = lax.broadcasted_iota(jnp.int32, (c, c), 1)
    causal = si <= ti

    o_parts, s_new = [], []
    for h in range(HEADS):
        vh = v[:, h * DV:(h + 1) * DV]
        att = att_off[h] + jnp.where(causal, att_diag[h], 0.0)
        lhs = jnp.concatenate([q_dec[h], att.astype(BF16)], axis=1)
        rhs = jnp.concatenate([s_list[h].astype(BF16), vh], axis=0)
        o_parts.append(_dot(lhs, rhs))
        dcol = jnp.broadcast_to(d_last[h], (DK, DK)).T
        kt = k_dec[h].T.astype(BF16)
        s_new.append(jnp.concatenate([dcol, dcol], axis=1) * s_list[h] + _dot(kt, vh))
    return jnp.concatenate(o_parts, axis=1), s_new


def _tri_const(c):
    return jnp.asarray(np.tril(np.ones((c, c), np.float32)), BF16)


def _conv_tile(u_ref, xs_ref, hist_ref, cvs_ref, wdwb_ref, bdwb_ref, tt, tick=lambda: None):
    seg = tt // SUBLANES
    x0 = HIST * SUBLANES
    sub7 = lax.broadcasted_iota(jnp.int32, (SUBLANES, LANES), 0) == SUBLANES - 1
    grp = SUBLANES
    for cb in range(N_LB):
        cs = slice(cb * LANES, (cb + 1) * LANES)
        for s in range(SUBLANES):
            for a0 in range(0, seg, SUBLANES):
                val = u_ref[seg * s + a0:seg * s + a0 + SUBLANES, cs]
                xs_ref[cb, pl.ds(x0 + SUBLANES * a0 + s, SUBLANES, stride=SUBLANES), :] = val
        for a in range(seg - HIST, seg):
            e = a - (seg - HIST)
            new = xs_ref[cb, x0 + SUBLANES * a:x0 + SUBLANES * (a + 1), :]
            old = hist_ref[cb, SUBLANES * e:SUBLANES * (e + 1), :]
            xs_ref[cb, SUBLANES * e:SUBLANES * (e + 1), :] = pltpu.roll(jnp.where(sub7, old, new), 1, axis=0)
            hist_ref[cb, SUBLANES * e:SUBLANES * (e + 1), :] = new
        w = [wdwb_ref[SUBLANES * j:SUBLANES * (j + 1), cs] for j in range(CONV_W)]
        for g0 in range(0, seg, grp):
            if g0 % (seg // 2) == 0:
                tick()
            acc = [bdwb_ref[:, cs] for _ in range(grp)]
            for d in range(g0, g0 + grp + HIST):
                x = xs_ref[cb, SUBLANES * d:SUBLANES * (d + 1), :]
                for i in range(grp):
                    j = d - (g0 + i)
                    if 0 <= j < CONV_W:
                        acc[i] = acc[i] + w[j] * x
            for i in range(grp):
                cvs_ref[cb, pl.ds(g0 + i, SUBLANES, stride=seg), :] = acc[i]


def _split_wsq(wsq_ref):
    return wsq_ref.at[0], wsq_ref.at[1], wsq_ref.at[2], wsq_ref.at[3]


def _prompt_kernel(xn_ref, xb_ref, pb_ref, gpre_ref, wa_ref, wb_ref, wr_ref, wup_ref, bup_ref, tri_ref,
                   wdwb_ref, bdwb_ref, ggla_ref, gln_ref, bln_ref, wsq_ref, wpe_ref,
                   gfin_ref,
                   y_ref, s_ref, cs_ref,
                   q_ref, k_ref, la_ref, v_ref, h_ref, o_ref, xs_ref, hist_ref, cvs_ref, sst_ref,
                   st_yb_ref, st_ma_ref, st_sgb_ref, u_ref, sza_ref, szb_ref, mg_ref, x1_ref, x1b_ref,
                   *, tt, n_t, n_tiles):
    wao_ref, wbo_ref, wo_ref, wpg_ref = _split_wsq(wsq_ref)
    i = pl.program_id(0)
    live = i < n_tiles
    t = jnp.minimum(i, n_tiles - 1) % n_t
    cur = i % 2
    nxt = 1 - cur

    @pl.when(jnp.logical_and(t == 0, live))
    def _():
        sst_ref[...] = jnp.zeros(sst_ref.shape, F32)
        hist_ref[...] = jnp.zeros(hist_ref.shape, F32)

    @pl.when(i == 0)
    def _():
        st_yb_ref[...] = jnp.zeros(st_yb_ref.shape, BF16)
        st_ma_ref[...] = jnp.zeros(st_ma_ref.shape, F32)
        st_sgb_ref[...] = jnp.zeros(st_sgb_ref.shape, F32)
        h0 = _rms(xb_ref[0], gpre_ref[...]).astype(BF16)
        h_ref[0] = h0
        q_ref[0], k_ref[0], la_ref[0], v_ref[0] = _qkv_la(h0, wa_ref, wr_ref, wup_ref, bup_ref)

    c = D_MODEL
    za0 = 2 * KEY_DIM + VAL_DIM

    def sliced(store, fn, width=D_MODEL):
        return [functools.partial(lambda lo: store(lo, lo + MXU_SLICE, fn(lo, lo + MXU_SLICE)), n * MXU_SLICE)
                for n in range(width // MXU_SLICE)]

    def into(ref, *lead):
        def store(lo, hi, val):
            ref[(*lead, slice(None), slice(lo, hi))] = val
        return store

    def ticker(jobs, n_ticks):
        state = [0, 0]

        def tick():
            state[0] += 1
            upto = -(-len(jobs) * state[0] // n_ticks)
            while state[1] < min(upto, len(jobs)):
                jobs[state[1]]()
                state[1] += 1
        return tick

    def store_x1(lo, hi, val):
        x1_ref[:, lo:hi] = val
        x1b_ref[:, lo:hi] = val.astype(BF16)

    def back_jobs():
        pb = pb_ref[0].astype(BF16)
        back1 = sliced(into(mg_ref), lambda lo, hi: (
            st_ma_ref[:, lo:hi] + st_sgb_ref[:, lo:hi] * _dot(st_yb_ref[...], wbo_ref[:, lo:hi])).astype(BF16))
        back2 = sliced(store_x1, lambda lo, hi: xb_ref[0, :, lo:hi] + _dot(mg_ref[...], wo_ref[:, lo:hi]))
        back3 = sliced(into(y_ref, 0), lambda lo, hi: (
            x1_ref[:, lo:hi] + _sigmoid(_dot(x1b_ref[...], wpg_ref[:, lo:hi])) * _dot(pb, wpe_ref[:, lo:hi])))
        return back1, back2, back3

    @pl.when(live)
    def _():
        h = h_ref[cur]
        hn = _rms(xn_ref[0], gpre_ref[...]).astype(BF16)
        h_ref[nxt] = hn

        def store_la(lo, hi, r):
            la_ref[nxt] = _log_sigmoid(_dot(r.astype(BF16), wup_ref[...]) + bup_ref[...]) * (LOG2E / TAU)

        ahead = (sliced(into(q_ref, nxt), lambda lo, hi: _dot(hn, wa_ref[:, lo:hi]) * (DK ** -0.5), KEY_DIM)
                 + sliced(into(k_ref, nxt), lambda lo, hi: _dot(hn, wa_ref[:, KEY_DIM + lo:KEY_DIM + hi]),
                          KEY_DIM)
                 + sliced(into(v_ref, nxt), lambda lo, hi: _dot(
                     hn, wa_ref[:, 2 * KEY_DIM + lo:2 * KEY_DIM + hi]).astype(BF16), VAL_DIM)
                 + sliced(store_la, lambda lo, hi: _dot(hn, wr_ref[...]), MXU_SLICE))

        back1, back2, back3 = back_jobs()
        n_chunks = tt // CHUNK
        jobs = (sliced(into(u_ref), lambda lo, hi: (_dot(h, wb_ref[:, lo:hi])
                                                    * _sigmoid(_dot(h, wb_ref[:, c + lo:c + hi]))))
                + back1 + sliced(into(sza_ref), lambda lo, hi: _silu(_dot(h, wa_ref[:, za0 + lo:za0 + hi])))
                + back2 + ahead)
        tick = ticker(jobs, n_chunks * HEADS)
        for ci in range(n_chunks):
            rows = slice(ci * CHUNK, (ci + 1) * CHUNK)
            s_list = [sst_ref[hd] for hd in range(HEADS)]
            o, s_new = _gla_chunk(q_ref[cur, rows, :], k_ref[cur, rows, :], la_ref[cur, rows, :],
                                  v_ref[cur, rows, :], s_list, tri_ref[...], CHUNK, tick)
            o_ref[rows, :] = o
            for hd in range(HEADS):
                sst_ref[hd] = s_new[hd]

        ona = _gated_heads(o_ref[...], sza_ref[...], ggla_ref)
        jobs = (sliced(into(st_ma_ref), lambda lo, hi: _sigmoid(_dot(h, wb_ref[:, 3 * c + lo:3 * c + hi])))
                + back3
                + sliced(into(st_ma_ref), lambda lo, hi: st_ma_ref[:, lo:hi] * _dot(ona, wao_ref[:, lo:hi]))
                + sliced(into(szb_ref), lambda lo, hi: _silu(_dot(h, wb_ref[:, 2 * c + lo:2 * c + hi])))
                + sliced(into(st_sgb_ref), lambda lo, hi: _sigmoid(_dot(h, wb_ref[:, 4 * c + lo:4 * c + hi]))))
        tick = ticker(jobs, 2 * N_LB)
        _conv_tile(u_ref, xs_ref, hist_ref, cvs_ref, wdwb_ref, bdwb_ref, tt, tick)
        y_ref[0] = _rms(y_ref[0], gfin_ref[...])

        cv = jnp.concatenate([cvs_ref[cb] for cb in range(N_LB)], axis=1)
        st_yb_ref[...] = _conv_branch_in(cv, szb_ref[...], gln_ref, bln_ref)
        s_ref[0] = sst_ref[...]
        cs_ref[0] = u_ref[tt - HIST:tt, :]

    @pl.when(jnp.logical_not(live))
    def _():
        for job in sum(back_jobs(), []):
            job()
        y_ref[0] = _rms(y_ref[0], gfin_ref[...])


def _whole():
    return pl.BlockSpec(memory_space=pltpu.VMEM)


def _prompt_call(x, p, w, tt):
    bsz, t_len, _ = x.shape
    seg = tt // SUBLANES
    assert t_len % tt == 0 and tt % CHUNK == 0 and seg >= HIST and seg % SUBLANES == 0
    n_t = t_len // tt
    n_tiles = bsz * n_t
    front = lambda i: jnp.minimum(i, n_tiles - 1)
    back = lambda i: jnp.maximum(i - 1, 0)
    ahead = lambda i: jnp.minimum(i + 1, n_tiles - 1)
    tok = lambda wd, tile: pl.BlockSpec((1, tt, wd), lambda i: (tile(i) // n_t, tile(i) % n_t, 0))
    names = ["gpre", "wa", "wb", "wr", "wup", "bup", "tri", "wdwb", "bdwb", "ggla", "gln", "bln", "wsq",
             "wpe", "gfin"]
    return pl.pallas_call(
        functools.partial(_prompt_kernel, tt=tt, n_t=n_t, n_tiles=n_tiles),
        grid=(n_tiles + 1,),
        in_specs=[tok(D_MODEL, ahead), tok(D_MODEL, back), tok(PLE, back)] + [_whole()] * len(names),
        out_specs=[tok(D_MODEL, back),
                   pl.BlockSpec((1, HEADS, DK, DV), lambda i: (front(i) // n_t, 0, 0, 0)),
                   pl.BlockSpec((1, HIST, D_MODEL), lambda i: (front(i) // n_t, 0, 0))],
        out_shape=[jax.ShapeDtypeStruct((bsz, t_len, D_MODEL), F32),
                   jax.ShapeDtypeStruct((bsz, HEADS, DK, DV), F32),
                   jax.ShapeDtypeStruct((bsz, HIST, D_MODEL), F32)],
        scratch_shapes=[pltpu.VMEM((2, tt, KEY_DIM), F32), pltpu.VMEM((2, tt, KEY_DIM), F32),
                        pltpu.VMEM((2, tt, KEY_DIM), F32), pltpu.VMEM((2, tt, VAL_DIM), BF16),
                        pltpu.VMEM((2, tt, D_MODEL), BF16),
                        pltpu.VMEM((tt, VAL_DIM), F32),
                        pltpu.VMEM((N_LB, (HIST + seg) * SUBLANES, LANES), F32),
                        pltpu.VMEM((N_LB, HIST * SUBLANES, LANES), F32),
                        pltpu.VMEM((N_LB, tt, LANES), F32),
                        pltpu.VMEM((HEADS, DK, DV), F32),
                        pltpu.VMEM((tt, D_MODEL), BF16),
                        pltpu.VMEM((tt, D_MODEL), F32), pltpu.VMEM((tt, D_MODEL), F32),
                        pltpu.VMEM((tt, D_MODEL), F32), pltpu.VMEM((tt, D_MODEL), F32),
                        pltpu.VMEM((tt, D_MODEL), F32), pltpu.VMEM((tt, D_MODEL), BF16),
                        pltpu.VMEM((tt, D_MODEL), F32), pltpu.VMEM((tt, D_MODEL), BF16)],
        compiler_params=pltpu.CompilerParams(dimension_semantics=("arbitrary",),
                                             vmem_limit_bytes=VMEM_LIMIT),
        name="prompt_layer",
    )(x, x, p, *[w[n] for n in names])


def _pre_kernel(x_ref, gpre_ref, wa_ref, wb_ref, wr_ref, wup_ref, bup_ref,
                q_ref, k_ref, la_ref, v_ref, u_ref):
    h = _rms(x_ref[...], gpre_ref[...]).astype(BF16)
    q_ref[...], k_ref[...], la_ref[...], v_ref[...] = _qkv_la(h, wa_ref, wr_ref, wup_ref, bup_ref)
    u_ref[...] = _glu(h, wb_ref)


def _pre_call(x2d, w, tm):
    n = x2d.shape[0]
    row = lambda wd: pl.BlockSpec((tm, wd), lambda i: (i, 0))
    outs = [(KEY_DIM, F32), (KEY_DIM, F32), (KEY_DIM, F32), (VAL_DIM, BF16), (D_MODEL, F32)]
    names = ["gpre", "wa", "wb", "wr", "wup", "bup"]
    return pl.pallas_call(
        _pre_kernel,
        grid=(n // tm,),
        in_specs=[row(D_MODEL)] + [_whole()] * len(names),
        out_specs=[row(wd) for wd, _ in outs],
        out_shape=[jax.ShapeDtypeStruct((n, wd), dt) for wd, dt in outs],
        compiler_params=pltpu.CompilerParams(dimension_semantics=("arbitrary",),
                                             vmem_limit_bytes=VMEM_LIMIT),
        name="sample_pre",
    )(x2d, *[w[n_] for n_ in names])


def _sample_conv_kernel(c0_ref, u_ref, wdwb_ref, bdwb_ref, cv_ref, cs_ref, *, t_len, nbs):
    ext = lambda d, rows, cs: c0_ref[d, rows, cs] if d < HIST else u_ref[d - HIST, rows, cs]
    for r0 in range(0, nbs, SUBLANES):
        rows = slice(r0, r0 + SUBLANES)
        for cb in range(N_LB):
            cs = slice(cb * LANES, (cb + 1) * LANES)
            acc = [bdwb_ref[:, cs] for _ in range(t_len)]
            for d in range(HIST + t_len):
                x = ext(d, rows, cs)
                for t in range(t_len):
                    j = d - t
                    if 0 <= j < CONV_W:
                        acc[t] = acc[t] + wdwb_ref[SUBLANES * j:SUBLANES * (j + 1), cs] * x
            for t in range(t_len):
                cv_ref[t, rows, cs] = acc[t]
    for d in range(HIST):
        cs_ref[d] = ext(d + t_len, slice(None), slice(None))


def _sample_conv_call(c0t, ut, w, nbs):
    _, bsz, _ = c0t.shape
    t_len = ut.shape[0]
    assert bsz % nbs == 0 and nbs % SUBLANES == 0
    blk = lambda rows: pl.BlockSpec((rows, nbs, D_MODEL), lambda i: (0, i, 0))
    return pl.pallas_call(
        functools.partial(_sample_conv_kernel, t_len=t_len, nbs=nbs),
        grid=(bsz // nbs,),
        in_specs=[blk(HIST), blk(t_len), _whole(), _whole()],
        out_specs=[blk(t_len), blk(HIST)],
        out_shape=[jax.ShapeDtypeStruct((t_len, bsz, D_MODEL), F32),
                   jax.ShapeDtypeStruct((HIST, bsz, D_MODEL), F32)],
        compiler_params=pltpu.CompilerParams(dimension_semantics=("arbitrary",),
                                             vmem_limit_bytes=VMEM_LIMIT),
        name="sample_conv",
    )(c0t, ut, w["wdwb"], w["bdwb"])


def _mix_sample_kernel(q_ref, k_ref, la_ref, v_ref, s0_ref, tri_ref, o_ref, s_ref, *, nb, t_len):
    pad = PACK_ROWS - t_len

    def padf(z):
        zf = z.astype(F32)
        return jnp.concatenate([zf, jnp.zeros((pad, z.shape[1]), F32)], axis=0).astype(z.dtype)

    def pair(m, carry):
        for slot in range(SEQ_UNROLL):
            n = SEQ_UNROLL * m + slot
            s_list = [s0_ref[n, hd] for hd in range(HEADS)]
            o, s_new = _gla_chunk(padf(q_ref[n]), padf(k_ref[n]), padf(la_ref[n]), padf(v_ref[n]),
                                  s_list, tri_ref[...], t_len)
            o_ref[n] = o[0:t_len]
            for hd in range(HEADS):
                s_ref[n, hd] = s_new[hd]
        return carry

    lax.fori_loop(0, nb // SEQ_UNROLL, pair, 0)


def _mix_sample_call(q, k, la, v, s0, nb):
    bsz, t_len, _ = q.shape
    assert bsz % nb == 0 and nb % SEQ_UNROLL == 0 and t_len <= SUB
    tok = lambda wd: pl.BlockSpec((nb, t_len, wd), lambda i: (i, 0, 0))
    st = pl.BlockSpec((nb, HEADS, DK, DV), lambda i: (i, 0, 0, 0))
    return pl.pallas_call(
        functools.partial(_mix_sample_kernel, nb=nb, t_len=t_len),
        grid=(bsz // nb,),
        in_specs=[tok(KEY_DIM), tok(KEY_DIM), tok(KEY_DIM), tok(VAL_DIM), st, _whole()],
        out_specs=[tok(VAL_DIM), st],
        out_shape=[jax.ShapeDtypeStruct((bsz, t_len, VAL_DIM), F32),
                   jax.ShapeDtypeStruct((bsz, HEADS, DK, DV), F32)],
        compiler_params=pltpu.CompilerParams(dimension_semantics=("arbitrary",),
                                             vmem_limit_bytes=VMEM_LIMIT),
        name="sample_mix",
    )(q, k, la, v, s0, _tri_const(PACK_ROWS))


def _post_kernel(x_ref, o_ref, cv_ref, p_ref, gpre_ref, wa_ref, wb_ref,
                 ggla_ref, gln_ref, bln_ref, wsq_ref, wpe_ref, gfin_ref, y_ref):
    wao_ref, wbo_ref, wo_ref, wpg_ref = _split_wsq(wsq_ref)
    x = x_ref[...]
    c = D_MODEL
    h = _rms(x, gpre_ref[...]).astype(BF16)
    y_a = _dot(_gated_heads(o_ref[...], _z_a(h, wa_ref), ggla_ref), wao_ref[...])
    szb = _silu(_dot(h, wb_ref[:, 2 * c:3 * c]))
    sga = _sigmoid(_dot(h, wb_ref[:, 3 * c:4 * c]))
    sgb = _sigmoid(_dot(h, wb_ref[:, 4 * c:5 * c]))
    pe = _dot(p_ref[...].astype(BF16), wpe_ref[...])
    y_ref[...] = _tail(x, y_a, cv_ref[...], szb, sga, sgb, pe, gln_ref, bln_ref, wbo_ref, wo_ref, wpg_ref, gfin_ref)


def _post_call(x2d, o, cv, p2d, w, tm):
    n = x2d.shape[0]
    row = lambda wd: pl.BlockSpec((tm, wd), lambda i: (i, 0))
    names = ["gpre", "wa", "wb", "ggla", "gln", "bln", "wsq", "wpe", "gfin"]
    return pl.pallas_call(
        _post_kernel,
        grid=(n // tm,),
        in_specs=[row(D_MODEL)] * 3 + [row(PLE)] + [_whole()] * len(names),
        out_specs=row(D_MODEL),
        out_shape=jax.ShapeDtypeStruct((n, D_MODEL), F32),
        compiler_params=pltpu.CompilerParams(dimension_semantics=("arbitrary",),
                                             vmem_limit_bytes=VMEM_LIMIT),
        name="sample_post",
    )(x2d, o, cv, p2d, *[w[n_] for n_ in names])


def _sample_layer(x, p, s0, c0, w, tm, nb, nbs):
    bsz, t_len, _ = x.shape
    x2d = x.reshape(bsz * t_len, D_MODEL)
    q, k, la, v, u = _pre_call(x2d, w, tm)
    r3 = lambda z: z.reshape(bsz, t_len, z.shape[-1])
    tmaj = lambda z: jnp.transpose(z, (1, 0, 2))
    o, s_new = _mix_sample_call(r3(q), r3(k), r3(la), r3(v), s0, nb)
    cv_t, c_new_t = _sample_conv_call(tmaj(c0), tmaj(r3(u)), w, nbs)
    y = _post_call(x2d, o.reshape(-1, VAL_DIM), tmaj(cv_t).reshape(-1, D_MODEL), p.reshape(bsz * t_len, PLE), w, tm)
    return y.reshape(bsz, t_len, D_MODEL), s_new, tmaj(c_new_t)


def kernel(x_prompt, x_sample, state_gla, state_conv, p_prompt, p_sample, g_pre, w_in, w_a_up, b_a_up, g_gla, w_a_out, w_dw, b_dw, g_ln, b_ln, w_b_out, w_o, w_pe, w_pg, g_final):
    depth = w_in.shape[0]
    assert depth == 1, "one trunk layer"
    i = 0
    tl = _tiles()
    n_a = 2 * KEY_DIM + 2 * VAL_DIM
    row = lambda z: z.reshape(1, -1)
    w = {
        "gpre": row(g_pre[i]),
        "wa": w_in[i][:, :n_a].astype(BF16),
        "wb": w_in[i][:, n_a + RANK:].astype(BF16),
        "wr": jnp.pad(w_in[i][:, n_a:n_a + RANK], ((0, 0), (0, LANES - RANK))).astype(BF16),
        "wup": jnp.pad(w_a_up[i], ((0, LANES - RANK), (0, 0))).astype(BF16),
        "bup": row(b_a_up[i]),
        "tri": _tri_const(CHUNK),
        "wdwb": jnp.repeat(w_dw[i], SUBLANES, axis=0),
        "bdwb": jnp.broadcast_to(row(b_dw[i]), (SUBLANES, D_MODEL)),
        "ggla": row(g_gla[i]), "gln": row(g_ln[i]), "bln": row(b_ln[i]),
        "wsq": jnp.stack([w_a_out[i], w_b_out[i], w_o[i], w_pg[i]]).astype(BF16),
        "wpe": w_pe[i].astype(BF16), "gfin": row(g_final),
    }
    yp, sp, cp = _prompt_call(x_prompt, p_prompt[i], w, tl["tt"])
    ys, ss, cs = _sample_layer(x_sample, p_sample[i], state_gla[i], state_conv[i], w, tl["tm"], tl["nb"],
                               tl["nbs"])
    return (yp, ys, sp[None], cp[None], ss[None], cs[None])
```

```python
import functools

import numpy as np
import jax
import jax.numpy as jnp
from jax import lax
from jax.experimental import pallas as pl
from jax.experimental.pallas import tpu as pltpu

F32 = jnp.float32
BF16 = jnp.bfloat16

D_MODEL = 1024
HEADS = 4
DK = 128
DV = 256
KEY_DIM = HEADS * DK
VAL_DIM = HEADS * DV
RANK = 16
TAU = 16.0
CHUNK = 64
SUB = 8
PACK_ROWS = 16
CONV_W = 31
HIST = CONV_W - 1
PLE = 256
EPS = 1e-6
LOG2E = 1.4426950408889634
LANES = 128
SUBLANES = 8
N_LB = D_MODEL // LANES
MXU_SLICE = 256
SEQ_UNROLL = 4
VMEM_LIMIT = 62 * 1024 * 1024


def _tiles():
    return dict(tt=256, tm=256, nb=8, nbs=16)


def _dot(a, b):
    return jnp.dot(a, b, preferred_element_type=F32)


def _sigmoid(x):
    return 1.0 / (1.0 + jnp.exp2(x * (-LOG2E)))


def _silu(x):
    return x * _sigmoid(x)


def _log_sigmoid(x):
    return -(jnp.maximum(-x, 0.0) + jnp.log(1.0 + jnp.exp(-jnp.abs(x))))


def _rms(x, g):
    return x * lax.rsqrt(jnp.mean(x * x, axis=-1, keepdims=True) + EPS) * g


def _qkv_la(h, wa_ref, wr_ref, wup_ref, bup_ref):
    q = _dot(h, wa_ref[:, 0:KEY_DIM]) * (DK ** -0.5)
    k = _dot(h, wa_ref[:, KEY_DIM:2 * KEY_DIM])
    v = _dot(h, wa_ref[:, 2 * KEY_DIM:2 * KEY_DIM + VAL_DIM]).astype(BF16)
    r = _dot(h, wr_ref[...]).astype(BF16)
    la = _log_sigmoid(_dot(r, wup_ref[...]) + bup_ref[...]) * (LOG2E / TAU)
    return q, k, la, v


def _z_a(h, wa_ref):
    return _silu(_dot(h, wa_ref[:, 2 * KEY_DIM + VAL_DIM:]))


def _glu(h, wb_ref):
    c = D_MODEL
    return _dot(h, wb_ref[:, 0:c]) * _sigmoid(_dot(h, wb_ref[:, c:2 * c]))


def _gated_heads(o, sza, ggla_ref):
    g = ggla_ref[...]
    on = jnp.concatenate([_rms(o[:, hd * DV:(hd + 1) * DV], g) for hd in range(HEADS)], axis=1)
    return (on * sza).astype(BF16)


def _conv_branch_in(cv, szb, gln_ref, bln_ref):
    mu = jnp.mean(cv, axis=-1, keepdims=True)
    xc = cv - mu
    ln = xc * lax.rsqrt(jnp.mean(xc * xc, axis=-1, keepdims=True) + EPS) * gln_ref[...] + bln_ref[...]
    return (_silu(ln) * szb).astype(BF16)


def _tail(x, y_a, cv, szb, sga, sgb, pe, gln_ref, bln_ref, wbo_ref, wo_ref, wpg_ref, gfin_ref):
    y_b = _dot(_conv_branch_in(cv, szb, gln_ref, bln_ref), wbo_ref[...])
    merged = sga * y_a + sgb * y_b
    x1 = x + _dot(merged.astype(BF16), wo_ref[...])
    x2 = x1 + _sigmoid(_dot(x1.astype(BF16), wpg_ref[...])) * pe
    return _rms(x2, gfin_ref[...])


def _gla_chunk(q, k, a, v, s_list, tri, n_keys, tick=lambda: None):
    c = q.shape[0]
    n_sub = c // SUB
    k_used = min(SUB, n_keys)
    a0 = a.astype(BF16)
    r1 = a - a0.astype(F32)
    a1 = r1.astype(BF16)
    a2 = (r1 - a1.astype(F32)).astype(BF16)
    b = _dot(tri, a0) + _dot(tri, a1) + _dot(tri, a2)

    att_off, att_diag, q_dec, k_dec, d_last = [], [], [], [], []
    for h in range(HEADS):
        sl = slice(h * DK, (h + 1) * DK)
        qh, kh, bh = q[:, sl], k[:, sl], b[:, sl]
        bl = bh[c - 1:c, :]
        q_dec.append((qh * jnp.exp2(bh)).astype(BF16))
        k_dec.append(kh * jnp.exp2(bl - bh))
        d_last.append(jnp.exp2(bl))
        zrow = jnp.zeros((SUB, DK), F32)
        rows = [jnp.zeros((SUB, c), F32)]
        for i in range(1, n_sub):
            lo = i * SUB
            if lo >= n_keys:
                rows.append(jnp.zeros((SUB, c), F32))
                continue
            br = bh[lo - 1:lo, :]
            qi = jnp.concatenate([qh[lo:lo + SUB] * jnp.exp2(bh[lo:lo + SUB] - br), zrow], axis=0).astype(BF16)
            ki = kh[:lo] * jnp.exp2(br - bh[:lo])
            kfull = jnp.concatenate([ki, jnp.zeros((c - lo, DK), F32)], axis=0).astype(BF16)
            rows.append(lax.dot_general(qi, kfull, (((1,), (1,)), ((), ())),
                                        preferred_element_type=F32)[0:SUB])
        att_off.append(jnp.concatenate(rows, axis=0))
        col = lax.broadcasted_iota(jnp.int32, (SUB, c), 1)
        blocks = []
        for lo in range(0, c, SUB):
            blk = jnp.zeros((SUB, c), F32)
            for s in range(k_used):
                if lo + s < n_keys:
                    qb, bb = qh[lo:lo + SUB], bh[lo:lo + SUB]
                    dec = jnp.exp2(jnp.minimum(bb - bh[lo + s:lo + s + 1, :], 0.0))
                    m = qb * dec * kh[lo + s:lo + s + 1, :]
                    blk = jnp.where(col == lo + s, jnp.sum(m, axis=-1, keepdims=True), blk)
            blocks.append(blk)
        att_diag.append(jnp.concatenate(blocks, axis=0))
        tick()

    ti = lax.broadcasted_iota(jnp.int32, (c, c), 0)
    si = lax.broadcasted_iota(jnp.int32, (c, c), 1)
    causal = si <= ti

    o_parts, s_new = [], []
    for h in range(HEADS):
        vh = v[:, h * DV:(h + 1) * DV]
        att = att_off[h] + jnp.where(causal, att_diag[h], 0.0)
        lhs = jnp.concatenate([q_dec[h], att.astype(BF16)], axis=1)
        rhs = jnp.concatenate([s_list[h].astype(BF16), vh], axis=0)
        o_parts.append(_dot(lhs, rhs))
        dcol = jnp.broadcast_to(d_last[h], (DK, DK)).T
        kt = k_dec[h].T.astype(BF16)
        s_new.append(jnp.concatenate([dcol, dcol], axis=1) * s_list[h] + _dot(kt, vh))
    return jnp.concatenate(o_parts, axis=1), s_new


def _tri_const(c):
    return jnp.asarray(np.tril(np.ones((c, c), np.float32)), BF16)


def _conv_tile(u_ref, xs_ref, hist_ref, cvs_ref, wdwb_ref, bdwb_ref, tt, tick=lambda: None):
    seg = tt // SUBLANES
    x0 = HIST * SUBLANES
    sub7 = lax.broadcasted_iota(jnp.int32, (SUBLANES, LANES), 0) == SUBLANES - 1
    grp = SUBLANES
    for cb in range(N_LB):
        cs = slice(cb * LANES, (cb + 1) * LANES)
        for s in range(SUBLANES):
            for a0 in range(0, seg, SUBLANES):
                val = u_ref[seg * s + a0:seg * s + a0 + SUBLANES, cs]
                xs_ref[cb, pl.ds(x0 + SUBLANES * a0 + s, SUBLANES, stride=SUBLANES), :] = val
        for a in range(seg - HIST, seg):
            e = a - (seg - HIST)
            new = xs_ref[cb, x0 + SUBLANES * a:x0 + SUBLANES * (a + 1), :]
            old = hist_ref[cb, SUBLANES * e:SUBLANES * (e + 1), :]
            xs_ref[cb, SUBLANES * e:SUBLANES * (e + 1), :] = pltpu.roll(jnp.where(sub7, old, new), 1, axis=0)
            hist_ref[cb, SUBLANES * e:SUBLANES * (e + 1), :] = new
        w = [wdwb_ref[SUBLANES * j:SUBLANES * (j + 1), cs] for j in range(CONV_W)]
        for g0 in range(0, seg, grp):
            if g0 % (seg // 2) == 0:
                tick()
            acc = [bdwb_ref[:, cs] for _ in range(grp)]
            for d in range(g0, g0 + grp + HIST):
                x = xs_ref[cb, SUBLANES * d:SUBLANES * (d + 1), :]
                for i in range(grp):
                    j = d - (g0 + i)
                    if 0 <= j < CONV_W:
                        acc[i] = acc[i] + w[j] * x
            for i in range(grp):
                cvs_ref[cb, pl.ds(g0 + i, SUBLANES, stride=seg), :] = acc[i]


def _split_wsq(wsq_ref):
    return wsq_ref.at[0], wsq_ref.at[1], wsq_ref.at[2], wsq_ref.at[3]


def _prompt_kernel(xn_ref, xb_ref, pb_ref, gpre_ref, wa_ref, wb_ref, wr_ref, wup_ref, bup_ref, tri_ref,
                   wdwb_ref, bdwb_ref, ggla_ref, gln_ref, bln_ref, wsq_ref, wpe_ref,
                   gfin_ref,
                   y_ref, s_ref, cs_ref,
                   q_ref, k_ref, la_ref, v_ref, h_ref, o_ref, xs_ref, hist_ref, cvs_ref, sst_ref,
                   st_yb_ref, st_ma_ref, st_sgb_ref, u_ref, sza_ref, szb_ref, mg_ref, x1_ref, x1b_ref, ona_ref,
                   *, tt, n_t, n_tiles):
    wao_ref, wbo_ref, wo_ref, wpg_ref = _split_wsq(wsq_ref)
    i = pl.program_id(0)
    live = i < n_tiles
    t = jnp.minimum(i, n_tiles - 1) % n_t
    cur = i % 2
    nxt = 1 - cur

    @pl.when(jnp.logical_and(t == 0, live))
    def _():
        sst_ref[...] = jnp.zeros(sst_ref.shape, F32)
        hist_ref[...] = jnp.zeros(hist_ref.shape, F32)

    @pl.when(i == 0)
    def _():
        st_yb_ref[...] = jnp.zeros(st_yb_ref.shape, BF16)
        st_ma_ref[...] = jnp.zeros(st_ma_ref.shape, F32)
        st_sgb_ref[...] = jnp.zeros(st_sgb_ref.shape, F32)
        h0 = _rms(xb_ref[0], gpre_ref[...]).astype(BF16)
        h_ref[0] = h0
        q_ref[0], k_ref[0], la_ref[0], v_ref[0] = _qkv_la(h0, wa_ref, wr_ref, wup_ref, bup_ref)

    c = D_MODEL
    za0 = 2 * KEY_DIM + VAL_DIM

    def sliced(store, fn, width=D_MODEL):
        return [functools.partial(lambda lo: store(lo, lo + MXU_SLICE, fn(lo, lo + MXU_SLICE)), n * MXU_SLICE)
                for n in range(width // MXU_SLICE)]

    def into(ref, *lead):
        def store(lo, hi, val):
            ref[(*lead, slice(None), slice(lo, hi))] = val
        return store

    def ticker(jobs, n_ticks):
        state = [0, 0]

        def tick():
            state[0] += 1
            upto = -(-len(jobs) * state[0] // n_ticks)
            while state[1] < min(upto, len(jobs)):
                jobs[state[1]]()
                state[1] += 1
        return tick

    def store_x1(lo, hi, val):
        x1_ref[:, lo:hi] = val
        x1b_ref[:, lo:hi] = val.astype(BF16)

    def back_jobs():
        pb = pb_ref[0].astype(BF16)
        back1 = sliced(into(mg_ref), lambda lo, hi: (
            st_ma_ref[:, lo:hi] + st_sgb_ref[:, lo:hi] * _dot(st_yb_ref[...], wbo_ref[:, lo:hi])).astype(BF16))
        back2 = sliced(store_x1, lambda lo, hi: xb_ref[0, :, lo:hi] + _dot(mg_ref[...], wo_ref[:, lo:hi]))
        back3 = sliced(into(y_ref, 0), lambda lo, hi: (
            x1_ref[:, lo:hi] + _sigmoid(_dot(x1b_ref[...], wpg_ref[:, lo:hi])) * _dot(pb, wpe_ref[:, lo:hi])))
        return back1, back2, back3

    @pl.when(live)
    def _():
        h = h_ref[cur]
        hn = _rms(xn_ref[0], gpre_ref[...]).astype(BF16)
        h_ref[nxt] = hn

        def store_la(lo, hi, r):
            la_ref[nxt] = _log_sigmoid(_dot(r.astype(BF16), wup_ref[...]) + bup_ref[...]) * (LOG2E / TAU)

        ahead = (sliced(into(q_ref, nxt), lambda lo, hi: _dot(hn, wa_ref[:, lo:hi]) * (DK ** -0.5), KEY_DIM)
                 + sliced(into(k_ref, nxt), lambda lo, hi: _dot(hn, wa_ref[:, KEY_DIM + lo:KEY_DIM + hi]),
                          KEY_DIM)
                 + sliced(into(v_ref, nxt), lambda lo, hi: _dot(
                     hn, wa_ref[:, 2 * KEY_DIM + lo:2 * KEY_DIM + hi]).astype(BF16), VAL_DIM)
                 + sliced(store_la, lambda lo, hi: _dot(hn, wr_ref[...]), MXU_SLICE))

        back1, back2, back3 = back_jobs()
        n_chunks = tt // CHUNK
        jobs = (sliced(into(u_ref), lambda lo, hi: (_dot(h, wb_ref[:, lo:hi])
                                                    * _sigmoid(_dot(h, wb_ref[:, c + lo:c + hi]))))
                + back1 + sliced(into(sza_ref), lambda lo, hi: _silu(_dot(h, wa_ref[:, za0 + lo:za0 + hi])))
                + back2 + ahead)
        tick = ticker(jobs, n_chunks * HEADS)
        for ci in range(n_chunks):
            rows = slice(ci * CHUNK, (ci + 1) * CHUNK)
            s_list = [sst_ref[hd] for hd in range(HEADS)]
            o, s_new = _gla_chunk(q_ref[cur, rows, :], k_ref[cur, rows, :], la_ref[cur, rows, :],
                                  v_ref[cur, rows, :], s_list, tri_ref[...], CHUNK, tick)
            o_ref[rows, :] = o
            for hd in range(HEADS):
                sst_ref[hd] = s_new[hd]

        ona_ref[...] = _gated_heads(o_ref[...], sza_ref[...], ggla_ref)

        def halves(fn):
            return [functools.partial(fn, slice(r0, r0 + tt // 2), n * MXU_SLICE, (n + 1) * MXU_SLICE)
                    for n in range(c // MXU_SLICE) for r0 in (0, tt // 2)]

        def job_sga_ya(rows, lo, hi):
            st_ma_ref[rows, lo:hi] = (_sigmoid(_dot(h_ref[cur, rows, :], wb_ref[:, 3 * c + lo:3 * c + hi]))
                                      * _dot(ona_ref[rows, :], wao_ref[:, lo:hi]))

        def job_x2(rows, lo, hi):
            y_ref[0, rows, lo:hi] = (x1_ref[rows, lo:hi]
                                     + _sigmoid(_dot(x1b_ref[rows, :], wpg_ref[:, lo:hi]))
                                     * _dot(pb_ref[0, rows, :].astype(BF16), wpe_ref[:, lo:hi]))

        def job_szb(rows, lo, hi):
            szb_ref[rows, lo:hi] = _silu(_dot(h_ref[cur, rows, :], wb_ref[:, 2 * c + lo:2 * c + hi]))

        def job_sgb(rows, lo, hi):
            st_sgb_ref[rows, lo:hi] = _sigmoid(_dot(h_ref[cur, rows, :], wb_ref[:, 4 * c + lo:4 * c + hi]))

        jobs = halves(job_x2) + halves(job_sga_ya) + halves(job_szb) + halves(job_sgb)
        tick = ticker(jobs, 2 * N_LB)
        _conv_tile(u_ref, xs_ref, hist_ref, cvs_ref, wdwb_ref, bdwb_ref, tt, tick)
        y_ref[0] = _rms(y_ref[0], gfin_ref[...])

        cv = jnp.concatenate([cvs_ref[cb] for cb in range(N_LB)], axis=1)
        st_yb_ref[...] = _conv_branch_in(cv, szb_ref[...], gln_ref, bln_ref)
        s_ref[0] = sst_ref[...]
        cs_ref[0] = u_ref[tt - HIST:tt, :]

    @pl.when(jnp.logical_not(live))
    def _():
        for job in sum(back_jobs(), []):
            job()
        y_ref[0] = _rms(y_ref[0], gfin_ref[...])


def _whole():
    return pl.BlockSpec(memory_space=pltpu.VMEM)


def _prompt_call(x, p, w, tt):
    bsz, t_len, _ = x.shape
    seg = tt // SUBLANES
    assert t_len % tt == 0 and tt % CHUNK == 0 and seg >= HIST and seg % SUBLANES == 0
    n_t = t_len // tt
    n_tiles = bsz * n_t
    front = lambda i: jnp.minimum(i, n_tiles - 1)
    back = lambda i: jnp.maximum(i - 1, 0)
    ahead = lambda i: jnp.minimum(i + 1, n_tiles - 1)
    tok = lambda wd, tile: pl.BlockSpec((1, tt, wd), lambda i: (tile(i) // n_t, tile(i) % n_t, 0))
    names = ["gpre", "wa", "wb", "wr", "wup", "bup", "tri", "wdwb", "bdwb", "ggla", "gln", "bln", "wsq",
             "wpe", "gfin"]
    return pl.pallas_call(
        functools.partial(_prompt_kernel, tt=tt, n_t=n_t, n_tiles=n_tiles),
        grid=(n_tiles + 1,),
        in_specs=[tok(D_MODEL, ahead), tok(D_MODEL, back), tok(PLE, back)] + [_whole()] * len(names),
        out_specs=[tok(D_MODEL, back),
                   pl.BlockSpec((1, HEADS, DK, DV), lambda i: (front(i) // n_t, 0, 0, 0)),
                   pl.BlockSpec((1, HIST, D_MODEL), lambda i: (front(i) // n_t, 0, 0))],
        out_shape=[jax.ShapeDtypeStruct((bsz, t_len, D_MODEL), F32),
                   jax.ShapeDtypeStruct((bsz, HEADS, DK, DV), F32),
                   jax.ShapeDtypeStruct((bsz, HIST, D_MODEL), F32)],
        scratch_shapes=[pltpu.VMEM((2, tt, KEY_DIM), F32), pltpu.VMEM((2, tt, KEY_DIM), F32),
                        pltpu.VMEM((2, tt, KEY_DIM), F32), pltpu.VMEM((2, tt, VAL_DIM), BF16),
                        pltpu.VMEM((2, tt, D_MODEL), BF16),
                        pltpu.VMEM((tt, VAL_DIM), F32),
                        pltpu.VMEM((N_LB, (HIST + seg) * SUBLANES, LANES), F32),
                        pltpu.VMEM((N_LB, HIST * SUBLANES, LANES), F32),
                        pltpu.VMEM((N_LB, tt, LANES), F32),
                        pltpu.VMEM((HEADS, DK, DV), F32),
                        pltpu.VMEM((tt, D_MODEL), BF16),
                        pltpu.VMEM((tt, D_MODEL), F32), pltpu.VMEM((tt, D_MODEL), F32),
                        pltpu.VMEM((tt, D_MODEL), F32), pltpu.VMEM((tt, D_MODEL), F32),
                        pltpu.VMEM((tt, D_MODEL), F32), pltpu.VMEM((tt, D_MODEL), BF16),
                        pltpu.VMEM((tt, D_MODEL), F32), pltpu.VMEM((tt, D_MODEL), BF16),
                        pltpu.VMEM((tt, D_MODEL), BF16)],
        compiler_params=pltpu.CompilerParams(dimension_semantics=("arbitrary",),
                                             vmem_limit_bytes=VMEM_LIMIT),
        name="prompt_layer",
    )(x, x, p, *[w[n] for n in names])


def _pre_kernel(x_ref, gpre_ref, wa_ref, wb_ref, wr_ref, wup_ref, bup_ref,
                q_ref, k_ref, la_ref, v_ref, u_ref):
    h = _rms(x_ref[...], gpre_ref[...]).astype(BF16)
    q_ref[...], k_ref[...], la_ref[...], v_ref[...] = _qkv_la(h, wa_ref, wr_ref, wup_ref, bup_ref)
    u_ref[...] = _glu(h, wb_ref)


def _pre_call(x2d, w, tm):
    n = x2d.shape[0]
    row = lambda wd: pl.BlockSpec((tm, wd), lambda i: (i, 0))
    outs = [(KEY_DIM, F32), (KEY_DIM, F32), (KEY_DIM, F32), (VAL_DIM, BF16), (D_MODEL, F32)]
    names = ["gpre", "wa", "wb", "wr", "wup", "bup"]
    return pl.pallas_call(
        _pre_kernel,
        grid=(n // tm,),
        in_specs=[row(D_MODEL)] + [_whole()] * len(names),
        out_specs=[row(wd) for wd, _ in outs],
        out_shape=[jax.ShapeDtypeStruct((n, wd), dt) for wd, dt in outs],
        compiler_params=pltpu.CompilerParams(dimension_semantics=("arbitrary",),
                                             vmem_limit_bytes=VMEM_LIMIT),
        name="sample_pre",
    )(x2d, *[w[n_] for n_ in names])


def _sample_conv_kernel(c0_ref, u_ref, wdwb_ref, bdwb_ref, cv_ref, cs_ref, *, t_len, nbs):
    ext = lambda d, rows, cs: c0_ref[d, rows, cs] if d < HIST else u_ref[d - HIST, rows, cs]
    for r0 in range(0, nbs, SUBLANES):
        rows = slice(r0, r0 + SUBLANES)
        for cb in range(N_LB):
            cs = slice(cb * LANES, (cb + 1) * LANES)
            acc = [bdwb_ref[:, cs] for _ in range(t_len)]
            for d in range(HIST + t_len):
                x = ext(d, rows, cs)
                for t in range(t_len):
                    j = d - t
                    if 0 <= j < CONV_W:
                        acc[t] = acc[t] + wdwb_ref[SUBLANES * j:SUBLANES * (j + 1), cs] * x
            for t in range(t_len):
                cv_ref[t, rows, cs] = acc[t]
    for d in range(HIST):
        cs_ref[d] = ext(d + t_len, slice(None), slice(None))


def _sample_conv_call(c0t, ut, w, nbs):
    _, bsz, _ = c0t.shape
    t_len = ut.shape[0]
    assert bsz % nbs == 0 and nbs % SUBLANES == 0
    blk = lambda rows: pl.BlockSpec((rows, nbs, D_MODEL), lambda i: (0, i, 0))
    return pl.pallas_call(
        functools.partial(_sample_conv_kernel, t_len=t_len, nbs=nbs),
        grid=(bsz // nbs,),
        in_specs=[blk(HIST), blk(t_len), _whole(), _whole()],
        out_specs=[blk(t_len), blk(HIST)],
        out_shape=[jax.ShapeDtypeStruct((t_len, bsz, D_MODEL), F32),
                   jax.ShapeDtypeStruct((HIST, bsz, D_MODEL), F32)],
        compiler_params=pltpu.CompilerParams(dimension_semantics=("arbitrary",),
                                             vmem_limit_bytes=VMEM_LIMIT),
        name="sample_conv",
    )(c0t, ut, w["wdwb"], w["bdwb"])


def _mix_sample_kernel(q_ref, k_ref, la_ref, v_ref, s0_ref, tri_ref, o_ref, s_ref, *, nb, t_len):
    pad = PACK_ROWS - t_len

    def padf(z):
        zf = z.astype(F32)
        return jnp.concatenate([zf, jnp.zeros((pad, z.shape[1]), F32)], axis=0).astype(z.dtype)

    def pair(m, carry):
        for slot in range(SEQ_UNROLL):
            n = SEQ_UNROLL * m + slot
            s_list = [s0_ref[n, hd] for hd in range(HEADS)]
            o, s_new = _gla_chunk(padf(q_ref[n]), padf(k_ref[n]), padf(la_ref[n]), padf(v_ref[n]),
                                  s_list, tri_ref[...], t_len)
            o_ref[n] = o[0:t_len]
            for hd in range(HEADS):
                s_ref[n, hd] = s_new[hd]
        return carry

    lax.fori_loop(0, nb // SEQ_UNROLL, pair, 0)


def _mix_sample_call(q, k, la, v, s0, nb):
    bsz, t_len, _ = q.shape
    assert bsz % nb == 0 and nb % SEQ_UNROLL == 0 and t_len <= SUB
    tok = lambda wd: pl.BlockSpec((nb, t_len, wd), lambda i: (i, 0, 0))
    st = pl.BlockSpec((nb, HEADS, DK, DV), lambda i: (i, 0, 0, 0))
    return pl.pallas_call(
        functools.partial(_mix_sample_kernel, nb=nb, t_len=t_len),
        grid=(bsz // nb,),
        in_specs=[tok(KEY_DIM), tok(KEY_DIM), tok(KEY_DIM), tok(VAL_DIM), st, _whole()],
        out_specs=[tok(VAL_DIM), st],
        out_shape=[jax.ShapeDtypeStruct((bsz, t_len, VAL_DIM), F32),
                   jax.ShapeDtypeStruct((bsz, HEADS, DK, DV), F32)],
        compiler_params=pltpu.CompilerParams(dimension_semantics=("arbitrary",),
                                             vmem_limit_bytes=VMEM_LIMIT),
        name="sample_mix",
    )(q, k, la, v, s0, _tri_const(PACK_ROWS))


def _post_kernel(x_ref, o_ref, cv_ref, p_ref, gpre_ref, wa_ref, wb_ref,
                 ggla_ref, gln_ref, bln_ref, wsq_ref, wpe_ref, gfin_ref, y_ref):
    wao_ref, wbo_ref, wo_ref, wpg_ref = _split_wsq(wsq_ref)
    x = x_ref[...]
    c = D_MODEL
    h = _rms(x, gpre_ref[...]).astype(BF16)
    y_a = _dot(_gated_heads(o_ref[...], _z_a(h, wa_ref), ggla_ref), wao_ref[...])
    szb = _silu(_dot(h, wb_ref[:, 2 * c:3 * c]))
    sga = _sigmoid(_dot(h, wb_ref[:, 3 * c:4 * c]))
    sgb = _sigmoid(_dot(h, wb_ref[:, 4 * c:5 * c]))
    pe = _dot(p_ref[...].astype(BF16), wpe_ref[...])
    y_ref[...] = _tail(x, y_a, cv_ref[...], szb, sga, sgb, pe, gln_ref, bln_ref, wbo_ref, wo_ref, wpg_ref, gfin_ref)


def _post_call(x2d, o, cv, p2d, w, tm):
    n = x2d.shape[0]
    row = lambda wd: pl.BlockSpec((tm, wd), lambda i: (i, 0))
    names = ["gpre", "wa", "wb", "ggla", "gln", "bln", "wsq", "wpe", "gfin"]
    return pl.pallas_call(
        _post_kernel,
        grid=(n // tm,),
        in_specs=[row(D_MODEL)] * 3 + [row(PLE)] + [_whole()] * len(names),
        out_specs=row(D_MODEL),
        out_shape=jax.ShapeDtypeStruct((n, D_MODEL), F32),
        compiler_params=pltpu.CompilerParams(dimension_semantics=("arbitrary",),
                                             vmem_limit_bytes=VMEM_LIMIT),
        name="sample_post",
    )(x2d, o, cv, p2d, *[w[n_] for n_ in names])


def _sample_layer(x, p, s0, c0, w, tm, nb, nbs):
    bsz, t_len, _ = x.shape
    x2d = x.reshape(bsz * t_len, D_MODEL)
    q, k, la, v, u = _pre_call(x2d, w, tm)
    r3 = lambda z: z.reshape(bsz, t_len, z.shape[-1])
    tmaj = lambda z: jnp.transpose(z, (1, 0, 2))
    o, s_new = _mix_sample_call(r3(q), r3(k), r3(la), r3(v), s0, nb)
    cv_t, c_new_t = _sample_conv_call(tmaj(c0), tmaj(r3(u)), w, nbs)
    y = _post_call(x2d, o.reshape(-1, VAL_DIM), tmaj(cv_t).reshape(-1, D_MODEL), p.reshape(bsz * t_len, PLE), w, tm)
    return y.reshape(bsz, t_len, D_MODEL), s_new, tmaj(c_new_t)


def kernel(x_prompt, x_sample, state_gla, state_conv, p_prompt, p_sample, g_pre, w_in, w_a_up, b_a_up, g_gla, w_a_out, w_dw, b_dw, g_ln, b_ln, w_b_out, w_o, w_pe, w_pg, g_final):
    depth = w_in.shape[0]
    assert depth == 1, "one trunk layer"
    i = 0
    tl = _tiles()
    n_a = 2 * KEY_DIM + 2 * VAL_DIM
    row = lambda z: z.reshape(1, -1)
    w = {
        "gpre": row(g_pre[i]),
        "wa": w_in[i][:, :n_a].astype(BF16),
        "wb": w_in[i][:, n_a + RANK:].astype(BF16),
        "wr": jnp.pad(w_in[i][:, n_a:n_a + RANK], ((0, 0), (0, LANES - RANK))).astype(BF16),
        "wup": jnp.pad(w_a_up[i], ((0, LANES - RANK), (0, 0))).astype(BF16),
        "bup": row(b_a_up[i]),
        "tri": _tri_const(CHUNK),
        "wdwb": jnp.repeat(w_dw[i], SUBLANES, axis=0),
        "bdwb": jnp.broadcast_to(row(b_dw[i]), (SUBLANES, D_MODEL)),
        "ggla": row(g_gla[i]), "gln": row(g_ln[i]), "bln": row(b_ln[i]),
        "wsq": jnp.stack([w_a_out[i], w_b_out[i], w_o[i], w_pg[i]]).astype(BF16),
        "wpe": w_pe[i].astype(BF16), "gfin": row(g_final),
    }
    yp, sp, cp = _prompt_call(x_prompt, p_prompt[i], w, tl["tt"])
    ys, ss, cs = _sample_layer(x_sample, p_sample[i], state_gla[i], state_conv[i], w, tl["tm"], tl["nb"],
                               tl["nbs"])
    return (yp, ys, sp[None], cp[None], ss[None], cs[None])
```

```python
import functools

import numpy as np
import jax
import jax.numpy as jnp
from jax import lax
from jax.experimental import pallas as pl
from jax.experimental.pallas import tpu as pltpu

F32 = jnp.float32
BF16 = jnp.bfloat16

D_MODEL = 1024
HEADS = 4
DK = 128
DV = 256
KEY_DIM = HEADS * DK
VAL_DIM = HEADS * DV
RANK = 16
TAU = 16.0
CHUNK = 64
SUB = 8
PACK_ROWS = 16
CONV_W = 31
HIST = CONV_W - 1
PLE = 256
EPS = 1e-6
LOG2E = 1.4426950408889634
LANES = 128
SUBLANES = 8
N_LB = D_MODEL // LANES
MXU_SLICE = 512
SEQ_UNROLL = 4
VMEM_LIMIT = 62 * 1024 * 1024


def _tiles():
    return dict(tt=256, tm=256, nb=8, nbs=16)


def _dot(a, b):
    return jnp.dot(a, b, preferred_element_type=F32)


def _sigmoid(x):
    return 1.0 / (1.0 + jnp.exp2(x * (-LOG2E)))


def _silu(x):
    return x * _sigmoid(x)


def _log_sigmoid(x):
    return -(jnp.maximum(-x, 0.0) + jnp.log(1.0 + jnp.exp(-jnp.abs(x))))


def _rms(x, g):
    return x * lax.rsqrt(jnp.mean(x * x, axis=-1, keepdims=True) + EPS) * g


def _qkv_la(h, wa_ref, wr_ref, wup_ref, bup_ref):
    q = _dot(h, wa_ref[:, 0:KEY_DIM]) * (DK ** -0.5)
    k = _dot(h, wa_ref[:, KEY_DIM:2 * KEY_DIM])
    v = _dot(h, wa_ref[:, 2 * KEY_DIM:2 * KEY_DIM + VAL_DIM]).astype(BF16)
    r = _dot(h, wr_ref[...]).astype(BF16)
    la = _log_sigmoid(_dot(r, wup_ref[...]) + bup_ref[...]) * (LOG2E / TAU)
    return q, k, la, v


def _z_a(h, wa_ref):
    return _silu(_dot(h, wa_ref[:, 2 * KEY_DIM + VAL_DIM:]))


def _glu(h, wb_ref):
    c = D_MODEL
    return _dot(h, wb_ref[:, 0:c]) * _sigmoid(_dot(h, wb_ref[:, c:2 * c]))


def _gated_heads(o, sza, ggla_ref):
    g = ggla_ref[...]
    on = jnp.concatenate([_rms(o[:, hd * DV:(hd + 1) * DV], g) for hd in range(HEADS)], axis=1)
    return (on * sza).astype(BF16)


def _conv_branch_in(cv, szb, gln_ref, bln_ref):
    mu = jnp.mean(cv, axis=-1, keepdims=True)
    xc = cv - mu
    ln = xc * lax.rsqrt(jnp.mean(xc * xc, axis=-1, keepdims=True) + EPS) * gln_ref[...] + bln_ref[...]
    return (_silu(ln) * szb).astype(BF16)


def _tail(x, y_a, cv, szb, sga, sgb, pe, gln_ref, bln_ref, wbo_ref, wo_ref, wpg_ref, gfin_ref):
    y_b = _dot(_conv_branch_in(cv, szb, gln_ref, bln_ref), wbo_ref[...])
    merged = sga * y_a + sgb * y_b
    x1 = x + _dot(merged.astype(BF16), wo_ref[...])
    x2 = x1 + _sigmoid(_dot(x1.astype(BF16), wpg_ref[...])) * pe
    return _rms(x2, gfin_ref[...])


def _gla_chunk(q, k, a, v, s_list, tri, n_keys, tick=lambda: None):
    c = q.shape[0]
    n_sub = c // SUB
    k_used = min(SUB, n_keys)
    a0 = a.astype(BF16)
    r1 = a - a0.astype(F32)
    a1 = r1.astype(BF16)
    a2 = (r1 - a1.astype(F32)).astype(BF16)
    b = _dot(tri, a0) + _dot(tri, a1) + _dot(tri, a2)

    att_off, att_diag, q_dec, k_dec, d_last = [], [], [], [], []
    for h in range(HEADS):
        sl = slice(h * DK, (h + 1) * DK)
        qh, kh, bh = q[:, sl], k[:, sl], b[:, sl]
        bl = bh[c - 1:c, :]
        q_dec.append((qh * jnp.exp2(bh)).astype(BF16))
        k_dec.append(kh * jnp.exp2(bl - bh))
        d_last.append(jnp.exp2(bl))
        zrow = jnp.zeros((SUB, DK), F32)
        rows = [jnp.zeros((SUB, c), F32)]
        for i in range(1, n_sub):
            lo = i * SUB
            if lo >= n_keys:
                rows.append(jnp.zeros((SUB, c), F32))
                continue
            br = bh[lo - 1:lo, :]
            qi = jnp.concatenate([qh[lo:lo + SUB] * jnp.exp2(bh[lo:lo + SUB] - br), zrow], axis=0).astype(BF16)
            ki = kh[:lo] * jnp.exp2(br - bh[:lo])
            kfull = jnp.concatenate([ki, jnp.zeros((c - lo, DK), F32)], axis=0).astype(BF16)
            rows.append(lax.dot_general(qi, kfull, (((1,), (1,)), ((), ())),
                                        preferred_element_type=F32)[0:SUB])
        att_off.append(jnp.concatenate(rows, axis=0))
        col = lax.broadcasted_iota(jnp.int32, (SUB, c), 1)
        blocks = []
        for lo in range(0, c, SUB):
            blk = jnp.zeros((SUB, c), F32)
            for s in range(k_used):
                if lo + s < n_keys:
                    qb, bb = qh[lo:lo + SUB], bh[lo:lo + SUB]
                    dec = jnp.exp2(jnp.minimum(bb - bh[lo + s:lo + s + 1, :], 0.0))
                    m = qb * dec * kh[lo + s:lo + s + 1, :]
                    blk = jnp.where(col == lo + s, jnp.sum(m, axis=-1, keepdims=True), blk)
            blocks.append(blk)
        att_diag.append(jnp.concatenate(blocks, axis=0))
        tick()

    ti = lax.broadcasted_iota(jnp.int32, (c, c), 0)
    si = lax.broadcasted_iota(jnp.int32, (c, c), 1)
    causal = si <= ti

    o_parts, s_new = [], []
    for h in range(HEADS):
        vh = v[:, h * DV:(h + 1) * DV]
        att = att_off[h] + jnp.where(causal, att_diag[h], 0.0)
        lhs = jnp.concatenate([q_dec[h], att.astype(BF16)], axis=1)
        rhs = jnp.concatenate([s_list[h].astype(BF16), vh], axis=0)
        o_parts.append(_dot(lhs, rhs))
        dcol = jnp.broadcast_to(d_last[h], (DK, DK)).T
        kt = k_dec[h].T.astype(BF16)
        s_new.append(jnp.concatenate([dcol, dcol], axis=1) * s_list[h] + _dot(kt, vh))
    return jnp.concatenate(o_parts, axis=1), s_new


def _tri_const(c):
    return jnp.asarray(np.tril(np.ones((c, c), np.float32)), BF16)


def _conv_tile(u_ref, xs_ref, hist_ref, cvs_ref, wdwb_ref, bdwb_ref, tt, tick=lambda: None):
    seg = tt // SUBLANES
    x0 = HIST * SUBLANES
    sub7 = lax.broadcasted_iota(jnp.int32, (SUBLANES, LANES), 0) == SUBLANES - 1
    grp = SUBLANES
    for cb in range(N_LB):
        cs = slice(cb * LANES, (cb + 1) * LANES)
        for s in range(SUBLANES):
            for a0 in range(0, seg, SUBLANES):
                val = u_ref[seg * s + a0:seg * s + a0 + SUBLANES, cs]
                xs_ref[cb, pl.ds(x0 + SUBLANES * a0 + s, SUBLANES, stride=SUBLANES), :] = val
        for a in range(seg - HIST, seg):
            e = a - (seg - HIST)
            new = xs_ref[cb, x0 + SUBLANES * a:x0 + SUBLANES * (a + 1), :]
            old = hist_ref[cb, SUBLANES * e:SUBLANES * (e + 1), :]
            xs_ref[cb, SUBLANES * e:SUBLANES * (e + 1), :] = pltpu.roll(jnp.where(sub7, old, new), 1, axis=0)
            hist_ref[cb, SUBLANES * e:SUBLANES * (e + 1), :] = new
        w = [wdwb_ref[SUBLANES * j:SUBLANES * (j + 1), cs] for j in range(CONV_W)]
        for g0 in range(0, seg, grp):
            if g0 % (seg // 2) == 0:
                tick()
            acc = [bdwb_ref[:, cs] for _ in range(grp)]
            for d in range(g0, g0 + grp + HIST):
                x = xs_ref[cb, SUBLANES * d:SUBLANES * (d + 1), :]
                for i in range(grp):
                    j = d - (g0 + i)
                    if 0 <= j < CONV_W:
                        acc[i] = acc[i] + w[j] * x
            for i in range(grp):
                cvs_ref[cb, pl.ds(g0 + i, SUBLANES, stride=seg), :] = acc[i]


def _split_wsq(wsq_ref):
    return wsq_ref.at[0], wsq_ref.at[1], wsq_ref.at[2], wsq_ref.at[3]


def _prompt_kernel(xn_ref, xb_ref, pb_ref, gpre_ref, wa_ref, wb_ref, wr_ref, wup_ref, bup_ref, tri_ref,
                   wdwb_ref, bdwb_ref, ggla_ref, gln_ref, bln_ref, wsq_ref, wpe_ref,
                   gfin_ref,
                   y_ref, s_ref, cs_ref,
                   q_ref, k_ref, la_ref, v_ref, h_ref, o_ref, xs_ref, hist_ref, cvs_ref, sst_ref,
                   st_yb_ref, st_ma_ref, st_sgb_ref, u_ref, sza_ref, szb_ref, mg_ref, x1_ref, x1b_ref,
                   *, tt, n_t, n_tiles):
    wao_ref, wbo_ref, wo_ref, wpg_ref = _split_wsq(wsq_ref)
    i = pl.program_id(0)
    live = i < n_tiles
    t = jnp.minimum(i, n_tiles - 1) % n_t
    cur = i % 2
    nxt = 1 - cur

    @pl.when(jnp.logical_and(t == 0, live))
    def _():
        sst_ref[...] = jnp.zeros(sst_ref.shape, F32)
        hist_ref[...] = jnp.zeros(hist_ref.shape, F32)

    @pl.when(i == 0)
    def _():
        st_yb_ref[...] = jnp.zeros(st_yb_ref.shape, BF16)
        st_ma_ref[...] = jnp.zeros(st_ma_ref.shape, F32)
        st_sgb_ref[...] = jnp.zeros(st_sgb_ref.shape, F32)
        h0 = _rms(xb_ref[0], gpre_ref[...]).astype(BF16)
        h_ref[0] = h0
        q_ref[0], k_ref[0], la_ref[0], v_ref[0] = _qkv_la(h0, wa_ref, wr_ref, wup_ref, bup_ref)

    c = D_MODEL
    za0 = 2 * KEY_DIM + VAL_DIM

    def sliced(store, fn, width=D_MODEL):
        return [functools.partial(lambda lo: store(lo, lo + MXU_SLICE, fn(lo, lo + MXU_SLICE)), n * MXU_SLICE)
                for n in range(width // MXU_SLICE)]

    def into(ref, *lead):
        def store(lo, hi, val):
            ref[(*lead, slice(None), slice(lo, hi))] = val
        return store

    def ticker(jobs, n_ticks):
        state = [0, 0]

        def tick():
            state[0] += 1
            upto = -(-len(jobs) * state[0] // n_ticks)
            while state[1] < min(upto, len(jobs)):
                jobs[state[1]]()
                state[1] += 1
        return tick

    def store_x1(lo, hi, val):
        x1_ref[:, lo:hi] = val
        x1b_ref[:, lo:hi] = val.astype(BF16)

    def back_jobs():
        pb = pb_ref[0].astype(BF16)
        back1 = sliced(into(mg_ref), lambda lo, hi: (
            st_ma_ref[:, lo:hi] + st_sgb_ref[:, lo:hi] * _dot(st_yb_ref[...], wbo_ref[:, lo:hi])).astype(BF16))
        back2 = sliced(store_x1, lambda lo, hi: xb_ref[0, :, lo:hi] + _dot(mg_ref[...], wo_ref[:, lo:hi]))
        back3 = sliced(into(y_ref, 0), lambda lo, hi: (
            x1_ref[:, lo:hi] + _sigmoid(_dot(x1b_ref[...], wpg_ref[:, lo:hi])) * _dot(pb, wpe_ref[:, lo:hi])))
        return back1, back2, back3

    @pl.when(live)
    def _():
        h = h_ref[cur]
        hn = _rms(xn_ref[0], gpre_ref[...]).astype(BF16)
        h_ref[nxt] = hn

        def store_la(lo, hi, r):
            la_ref[nxt] = _log_sigmoid(_dot(r.astype(BF16), wup_ref[...]) + bup_ref[...]) * (LOG2E / TAU)

        ahead = (sliced(into(q_ref, nxt), lambda lo, hi: _dot(hn, wa_ref[:, lo:hi]) * (DK ** -0.5), KEY_DIM)
                 + sliced(into(k_ref, nxt), lambda lo, hi: _dot(hn, wa_ref[:, KEY_DIM + lo:KEY_DIM + hi]),
                          KEY_DIM)
                 + sliced(into(v_ref, nxt), lambda lo, hi: _dot(
                     hn, wa_ref[:, 2 * KEY_DIM + lo:2 * KEY_DIM + hi]).astype(BF16), VAL_DIM)
                 + sliced(store_la, lambda lo, hi: _dot(hn, wr_ref[...]), MXU_SLICE))

        back1, back2, back3 = back_jobs()
        n_chunks = tt // CHUNK
        jobs = (sliced(into(u_ref), lambda lo, hi: (_dot(h, wb_ref[:, lo:hi])
                                                    * _sigmoid(_dot(h, wb_ref[:, c + lo:c + hi]))))
                + back1 + sliced(into(sza_ref), lambda lo, hi: _silu(_dot(h, wa_ref[:, za0 + lo:za0 + hi])))
                + back2 + ahead)
        tick = ticker(jobs, n_chunks * HEADS)
        for ci in range(n_chunks):
            rows = slice(ci * CHUNK, (ci + 1) * CHUNK)
            s_list = [sst_ref[hd] for hd in range(HEADS)]
            o, s_new = _gla_chunk(q_ref[cur, rows, :], k_ref[cur, rows, :], la_ref[cur, rows, :],
                                  v_ref[cur, rows, :], s_list, tri_ref[...], CHUNK, tick)
            o_ref[rows, :] = o
            for hd in range(HEADS):
                sst_ref[hd] = s_new[hd]

        ona = _gated_heads(o_ref[...], sza_ref[...], ggla_ref)
        jobs = (sliced(into(st_ma_ref), lambda lo, hi: _sigmoid(_dot(h, wb_ref[:, 3 * c + lo:3 * c + hi])))
                + back3
                + sliced(into(st_ma_ref), lambda lo, hi: st_ma_ref[:, lo:hi] * _dot(ona, wao_ref[:, lo:hi]))
                + sliced(into(szb_ref), lambda lo, hi: _silu(_dot(h, wb_ref[:, 2 * c + lo:2 * c + hi])))
                + sliced(into(st_sgb_ref), lambda lo, hi: _sigmoid(_dot(h, wb_ref[:, 4 * c + lo:4 * c + hi]))))
        tick = ticker(jobs, 2 * N_LB)
        _conv_tile(u_ref, xs_ref, hist_ref, cvs_ref, wdwb_ref, bdwb_ref, tt, tick)
        y_ref[0] = _rms(y_ref[0], gfin_ref[...])

        cv = jnp.concatenate([cvs_ref[cb] for cb in range(N_LB)], axis=1)
        st_yb_ref[...] = _conv_branch_in(cv, szb_ref[...], gln_ref, bln_ref)
        s_ref[0] = sst_ref[...]
        cs_ref[0] = u_ref[tt - HIST:tt, :]

    @pl.when(jnp.logical_not(live))
    def _():
        for job in sum(back_jobs(), []):
            job()
        y_ref[0] = _rms(y_ref[0], gfin_ref[...])


def _whole():
    return pl.BlockSpec(memory_space=pltpu.VMEM)


def _prompt_call(x, p, w, tt):
    bsz, t_len, _ = x.shape
    seg = tt // SUBLANES
    assert t_len % tt == 0 and tt % CHUNK == 0 and seg >= HIST and seg % SUBLANES == 0
    n_t = t_len // tt
    n_tiles = bsz * n_t
    front = lambda i: jnp.minimum(i, n_tiles - 1)
    back = lambda i: jnp.maximum(i - 1, 0)
    ahead = lambda i: jnp.minimum(i + 1, n_tiles - 1)
    tok = lambda wd, tile: pl.BlockSpec((1, tt, wd), lambda i: (tile(i) // n_t, tile(i) % n_t, 0))
    names = ["gpre", "wa", "wb", "wr", "wup", "bup", "tri", "wdwb", "bdwb", "ggla", "gln", "bln", "wsq",
             "wpe", "gfin"]
    return pl.pallas_call(
        functools.partial(_prompt_kernel, tt=tt, n_t=n_t, n_tiles=n_tiles),
        grid=(n_tiles + 1,),
        in_specs=[tok(D_MODEL, ahead), tok(D_MODEL, back), tok(PLE, back)] + [_whole()] * len(names),
        out_specs=[tok(D_MODEL, back),
                   pl.BlockSpec((1, HEADS, DK, DV), lambda i: (front(i) // n_t, 0, 0, 0)),
                   pl.BlockSpec((1, HIST, D_MODEL), lambda i: (front(i) // n_t, 0, 0))],
        out_shape=[jax.ShapeDtypeStruct((bsz, t_len, D_MODEL), F32),
                   jax.ShapeDtypeStruct((bsz, HEADS, DK, DV), F32),
                   jax.ShapeDtypeStruct((bsz, HIST, D_MODEL), F32)],
        scratch_shapes=[pltpu.VMEM((2, tt, KEY_DIM), F32), pltpu.VMEM((2, tt, KEY_DIM), F32),
                        pltpu.VMEM((2, tt, KEY_DIM), F32), pltpu.VMEM((2, tt, VAL_DIM), BF16),
                        pltpu.VMEM((2, tt, D_MODEL), BF16),
                        pltpu.VMEM((tt, VAL_DIM), F32),
                        pltpu.VMEM((N_LB, (HIST + seg) * SUBLANES, LANES), F32),
                        pltpu.VMEM((N_LB, HIST * SUBLANES, LANES), F32),
                        pltpu.VMEM((N_LB, tt, LANES), F32),
                        pltpu.VMEM((HEADS, DK, DV), F32),
                        pltpu.VMEM((tt, D_MODEL), BF16),
                        pltpu.VMEM((tt, D_MODEL), F32), pltpu.VMEM((tt, D_MODEL), F32),
                        pltpu.VMEM((tt, D_MODEL), F32), pltpu.VMEM((tt, D_MODEL), F32),
                        pltpu.VMEM((tt, D_MODEL), F32), pltpu.VMEM((tt, D_MODEL), BF16),
                        pltpu.VMEM((tt, D_MODEL), F32), pltpu.VMEM((tt, D_MODEL), BF16)],
        compiler_params=pltpu.CompilerParams(dimension_semantics=("arbitrary",),
                                             vmem_limit_bytes=VMEM_LIMIT),
        name="prompt_layer",
    )(x, x, p, *[w[n] for n in names])


def _pre_kernel(x_ref, gpre_ref, wa_ref, wb_ref, wr_ref, wup_ref, bup_ref,
                q_ref, k_ref, la_ref, v_ref, u_ref):
    h = _rms(x_ref[...], gpre_ref[...]).astype(BF16)
    q_ref[...], k_ref[...], la_ref[...], v_ref[...] = _qkv_la(h, wa_ref, wr_ref, wup_ref, bup_ref)
    u_ref[...] = _glu(h, wb_ref)


def _pre_call(x2d, w, tm):
    n = x2d.shape[0]
    row = lambda wd: pl.BlockSpec((tm, wd), lambda i: (i, 0))
    outs = [(KEY_DIM, F32), (KEY_DIM, F32), (KEY_DIM, F32), (VAL_DIM, BF16), (D_MODEL, F32)]
    names = ["gpre", "wa", "wb", "wr", "wup", "bup"]
    return pl.pallas_call(
        _pre_kernel,
        grid=(n // tm,),
        in_specs=[row(D_MODEL)] + [_whole()] * len(names),
        out_specs=[row(wd) for wd, _ in outs],
        out_shape=[jax.ShapeDtypeStruct((n, wd), dt) for wd, dt in outs],
        compiler_params=pltpu.CompilerParams(dimension_semantics=("arbitrary",),
                                             vmem_limit_bytes=VMEM_LIMIT),
        name="sample_pre",
    )(x2d, *[w[n_] for n_ in names])


def _sample_conv_kernel(c0_ref, u_ref, wdwb_ref, bdwb_ref, cv_ref, cs_ref, *, t_len, nbs):
    ext = lambda d, rows, cs: c0_ref[d, rows, cs] if d < HIST else u_ref[d - HIST, rows, cs]
    for r0 in range(0, nbs, SUBLANES):
        rows = slice(r0, r0 + SUBLANES)
        for cb in range(N_LB):
            cs = slice(cb * LANES, (cb + 1) * LANES)
            acc = [bdwb_ref[:, cs] for _ in range(t_len)]
            for d in range(HIST + t_len):
                x = ext(d, rows, cs)
                for t in range(t_len):
                    j = d - t
                    if 0 <= j < CONV_W:
                        acc[t] = acc[t] + wdwb_ref[SUBLANES * j:SUBLANES * (j + 1), cs] * x
            for t in range(t_len):
                cv_ref[t, rows, cs] = acc[t]
    for d in range(HIST):
        cs_ref[d] = ext(d + t_len, slice(None), slice(None))


def _sample_conv_call(c0t, ut, w, nbs):
    _, bsz, _ = c0t.shape
    t_len = ut.shape[0]
    assert bsz % nbs == 0 and nbs % SUBLANES == 0
    blk = lambda rows: pl.BlockSpec((rows, nbs, D_MODEL), lambda i: (0, i, 0))
    return pl.pallas_call(
        functools.partial(_sample_conv_kernel, t_len=t_len, nbs=nbs),
        grid=(bsz // nbs,),
        in_specs=[blk(HIST), blk(t_len), _whole(), _whole()],
        out_specs=[blk(t_len), blk(HIST)],
        out_shape=[jax.ShapeDtypeStruct((t_len, bsz, D_MODEL), F32),
                   jax.ShapeDtypeStruct((HIST, bsz, D_MODEL), F32)],
        compiler_params=pltpu.CompilerParams(dimension_semantics=("arbitrary",),
                                             vmem_limit_bytes=VMEM_LIMIT),
        name="sample_conv",
    )(c0t, ut, w["wdwb"], w["bdwb"])


def _mix_sample_kernel(q_ref, k_ref, la_ref, v_ref, s0_ref, tri_ref, o_ref, s_ref, *, nb, t_len):
    pad = PACK_ROWS - t_len

    def padf(z):
        zf = z.astype(F32)
        return jnp.concatenate([zf, jnp.zeros((pad, z.shape[1]), F32)], axis=0).astype(z.dtype)

    def pair(m, carry):
        for slot in range(SEQ_UNROLL):
            n = SEQ_UNROLL * m + slot
            s_list = [s0_ref[n, hd] for hd in range(HEADS)]
            o, s_new = _gla_chunk(padf(q_ref[n]), padf(k_ref[n]), padf(la_ref[n]), padf(v_ref[n]),
                                  s_list, tri_ref[...], t_len)
            o_ref[n] = o[0:t_len]
            for hd in range(HEADS):
                s_ref[n, hd] = s_new[hd]
        return carry

    lax.fori_loop(0, nb // SEQ_UNROLL, pair, 0)


def _mix_sample_call(q, k, la, v, s0, nb):
    bsz, t_len, _ = q.shape
    assert bsz % nb == 0 and nb % SEQ_UNROLL == 0 and t_len <= SUB
    tok = lambda wd: pl.BlockSpec((nb, t_len, wd), lambda i: (i, 0, 0))
    st = pl.BlockSpec((nb, HEADS, DK, DV), lambda i: (i, 0, 0, 0))
    return pl.pallas_call(
        functools.partial(_mix_sample_kernel, nb=nb, t_len=t_len),
        grid=(bsz // nb,),
        in_specs=[tok(KEY_DIM), tok(KEY_DIM), tok(KEY_DIM), tok(VAL_DIM), st, _whole()],
        out_specs=[tok(VAL_DIM), st],
        out_shape=[jax.ShapeDtypeStruct((bsz, t_len, VAL_DIM), F32),
                   jax.ShapeDtypeStruct((bsz, HEADS, DK, DV), F32)],
        compiler_params=pltpu.CompilerParams(dimension_semantics=("arbitrary",),
                                             vmem_limit_bytes=VMEM_LIMIT),
        name="sample_mix",
    )(q, k, la, v, s0, _tri_const(PACK_ROWS))


def _post_kernel(x_ref, o_ref, cv_ref, p_ref, gpre_ref, wa_ref, wb_ref,
                 ggla_ref, gln_ref, bln_ref, wsq_ref, wpe_ref, gfin_ref, y_ref):
    wao_ref, wbo_ref, wo_ref, wpg_ref = _split_wsq(wsq_ref)
    x = x_ref[...]
    c = D_MODEL
    h = _rms(x, gpre_ref[...]).astype(BF16)
    y_a = _dot(_gated_heads(o_ref[...], _z_a(h, wa_ref), ggla_ref), wao_ref[...])
    szb = _silu(_dot(h, wb_ref[:, 2 * c:3 * c]))
    sga = _sigmoid(_dot(h, wb_ref[:, 3 * c:4 * c]))
    sgb = _sigmoid(_dot(h, wb_ref[:, 4 * c:5 * c]))
    pe = _dot(p_ref[...].astype(BF16), wpe_ref[...])
    y_ref[...] = _tail(x, y_a, cv_ref[...], szb, sga, sgb, pe, gln_ref, bln_ref, wbo_ref, wo_ref, wpg_ref, gfin_ref)


def _post_call(x2d, o, cv, p2d, w, tm):
    n = x2d.shape[0]
    row = lambda wd: pl.BlockSpec((tm, wd), lambda i: (i, 0))
    names = ["gpre", "wa", "wb", "ggla", "gln", "bln", "wsq", "wpe", "gfin"]
    return pl.pallas_call(
        _post_kernel,
        grid=(n // tm,),
        in_specs=[row(D_MODEL)] * 3 + [row(PLE)] + [_whole()] * len(names),
        out_specs=row(D_MODEL),
        out_shape=jax.ShapeDtypeStruct((n, D_MODEL), F32),
        compiler_params=pltpu.CompilerParams(dimension_semantics=("arbitrary",),
                                             vmem_limit_bytes=VMEM_LIMIT),
        name="sample_post",
    )(x2d, o, cv, p2d, *[w[n_] for n_ in names])


def _sample_layer(x, p, s0, c0, w, tm, nb, nbs):
    bsz, t_len, _ = x.shape
    x2d = x.reshape(bsz * t_len, D_MODEL)
    q, k, la, v, u = _pre_call(x2d, w, tm)
    r3 = lambda z: z.reshape(bsz, t_len, z.shape[-1])
    tmaj = lambda z: jnp.transpose(z, (1, 0, 2))
    o, s_new = _mix_sample_call(r3(q), r3(k), r3(la), r3(v), s0, nb)
    cv_t, c_new_t = _sample_conv_call(tmaj(c0), tmaj(r3(u)), w, nbs)
    y = _post_call(x2d, o.reshape(-1, VAL_DIM), tmaj(cv_t).reshape(-1, D_MODEL), p.reshape(bsz * t_len, PLE), w, tm)
    return y.reshape(bsz, t_len, D_MODEL), s_new, tmaj(c_new_t)


def kernel(x_prompt, x_sample, state_gla, state_conv, p_prompt, p_sample, g_pre, w_in, w_a_up, b_a_up, g_gla, w_a_out, w_dw, b_dw, g_ln, b_ln, w_b_out, w_o, w_pe, w_pg, g_final):
    depth = w_in.shape[0]
    assert depth == 1, "one trunk layer"
    i = 0
    tl = _tiles()
    n_a = 2 * KEY_DIM + 2 * VAL_DIM
    row = lambda z: z.reshape(1, -1)
    w = {
        "gpre": row(g_pre[i]),
        "wa": w_in[i][:, :n_a].astype(BF16),
        "wb": w_in[i][:, n_a + RANK:].astype(BF16),
        "wr": jnp.pad(w_in[i][:, n_a:n_a + RANK], ((0, 0), (0, LANES - RANK))).astype(BF16),
        "wup": jnp.pad(w_a_up[i], ((0, LANES - RANK), (0, 0))).astype(BF16),
        "bup": row(b_a_up[i]),
        "tri": _tri_const(CHUNK),
        "wdwb": jnp.repeat(w_dw[i], SUBLANES, axis=0),
        "bdwb": jnp.broadcast_to(row(b_dw[i]), (SUBLANES, D_MODEL)),
        "ggla": row(g_gla[i]), "gln": row(g_ln[i]), "bln": row(b_ln[i]),
        "wsq": jnp.stack([w_a_out[i], w_b_out[i], w_o[i], w_pg[i]]).astype(BF16),
        "wpe": w_pe[i].astype(BF16), "gfin": row(g_final),
    }
    yp, sp, cp = _prompt_call(x_prompt, p_prompt[i], w, tl["tt"])
    ys, ss, cs = _sample_layer(x_sample, p_sample[i], state_gla[i], state_conv[i], w, tl["tm"], tl["nb"],
                               tl["nbs"])
    return (yp, ys, sp[None], cp[None], ss[None], cs[None])
```

```python
import functools

import numpy as np
import jax
import jax.numpy as jnp
from jax import lax
from jax.experimental import pallas as pl
from jax.experimental.pallas import tpu as pltpu

F32 = jnp.float32
BF16 = jnp.bfloat16

D_MODEL = 1024
HEADS = 4
DK = 128
DV = 256
KEY_DIM = HEADS * DK
VAL_DIM = HEADS * DV
RANK = 16
TAU = 16.0
CHUNK = 64
SUB = 8
PACK_ROWS = 16
CONV_W = 31
HIST = CONV_W - 1
PLE = 256
EPS = 1e-6
LOG2E = 1.4426950408889634
LANES = 128
SUBLANES = 8
N_LB = D_MODEL // LANES
MXU_SLICE = 1024
SEQ_UNROLL = 4
VMEM_LIMIT = 62 * 1024 * 1024


def _tiles():
    return dict(tt=256, tm=256, nb=8, nbs=16)


def _dot(a, b):
    return jnp.dot(a, b, preferred_element_type=F32)


def _sigmoid(x):
    return 1.0 / (1.0 + jnp.exp2(x * (-LOG2E)))


def _silu(x):
    return x * _sigmoid(x)


def _log_sigmoid(x):
    return -(jnp.maximum(-x, 0.0) + jnp.log(1.0 + jnp.exp(-jnp.abs(x))))


def _rms(x, g):
    return x * lax.rsqrt(jnp.mean(x * x, axis=-1, keepdims=True) + EPS) * g


def _qkv_la(h, wa_ref, wr_ref, wup_ref, bup_ref):
    q = _dot(h, wa_ref[:, 0:KEY_DIM]) * (DK ** -0.5)
    k = _dot(h, wa_ref[:, KEY_DIM:2 * KEY_DIM])
    v = _dot(h, wa_ref[:, 2 * KEY_DIM:2 * KEY_DIM + VAL_DIM]).astype(BF16)
    r = _dot(h, wr_ref[...]).astype(BF16)
    la = _log_sigmoid(_dot(r, wup_ref[...]) + bup_ref[...]) * (LOG2E / TAU)
    return q, k, la, v


def _z_a(h, wa_ref):
    return _silu(_dot(h, wa_ref[:, 2 * KEY_DIM + VAL_DIM:]))


def _glu(h, wb_ref):
    c = D_MODEL
    return _dot(h, wb_ref[:, 0:c]) * _sigmoid(_dot(h, wb_ref[:, c:2 * c]))


def _gated_heads(o, sza, ggla_ref):
    g = ggla_ref[...]
    on = jnp.concatenate([_rms(o[:, hd * DV:(hd + 1) * DV], g) for hd in range(HEADS)], axis=1)
    return (on * sza).astype(BF16)


def _conv_branch_in(cv, szb, gln_ref, bln_ref):
    mu = jnp.mean(cv, axis=-1, keepdims=True)
    xc = cv - mu
    ln = xc * lax.rsqrt(jnp.mean(xc * xc, axis=-1, keepdims=True) + EPS) * gln_ref[...] + bln_ref[...]
    return (_silu(ln) * szb).astype(BF16)


def _tail(x, y_a, cv, szb, sga, sgb, pe, gln_ref, bln_ref, wbo_ref, wo_ref, wpg_ref, gfin_ref):
    y_b = _dot(_conv_branch_in(cv, szb, gln_ref, bln_ref), wbo_ref[...])
    merged = sga * y_a + sgb * y_b
    x1 = x + _dot(merged.astype(BF16), wo_ref[...])
    x2 = x1 + _sigmoid(_dot(x1.astype(BF16), wpg_ref[...])) * pe
    return _rms(x2, gfin_ref[...])


def _gla_chunk(q, k, a, v, s_list, tri, n_keys, tick=lambda: None):
    c = q.shape[0]
    n_sub = c // SUB
    k_used = min(SUB, n_keys)
    a0 = a.astype(BF16)
    r1 = a - a0.astype(F32)
    a1 = r1.astype(BF16)
    a2 = (r1 - a1.astype(F32)).astype(BF16)
    b = _dot(tri, a0) + _dot(tri, a1) + _dot(tri, a2)

    att_off, att_diag, q_dec, k_dec, d_last = [], [], [], [], []
    for h in range(HEADS):
        sl = slice(h * DK, (h + 1) * DK)
        qh, kh, bh = q[:, sl], k[:, sl], b[:, sl]
        bl = bh[c - 1:c, :]
        q_dec.append((qh * jnp.exp2(bh)).astype(BF16))
        k_dec.append(kh * jnp.exp2(bl - bh))
        d_last.append(jnp.exp2(bl))
        zrow = jnp.zeros((SUB, DK), F32)
        rows = [jnp.zeros((SUB, c), F32)]
        for i in range(1, n_sub):
            lo = i * SUB
            if lo >= n_keys:
                rows.append(jnp.zeros((SUB, c), F32))
                continue
            br = bh[lo - 1:lo, :]
            qi = jnp.concatenate([qh[lo:lo + SUB] * jnp.exp2(bh[lo:lo + SUB] - br), zrow], axis=0).astype(BF16)
            ki = kh[:lo] * jnp.exp2(br - bh[:lo])
            kfull = jnp.concatenate([ki, jnp.zeros((c - lo, DK), F32)], axis=0).astype(BF16)
            rows.append(lax.dot_general(qi, kfull, (((1,), (1,)), ((), ())),
                                        preferred_element_type=F32)[0:SUB])
        att_off.append(jnp.concatenate(rows, axis=0))
        col = lax.broadcasted_iota(jnp.int32, (SUB, c), 1)
        blocks = []
        for lo in range(0, c, SUB):
            blk = jnp.zeros((SUB, c), F32)
            for s in range(k_used):
                if lo + s < n_keys:
                    qb, bb = qh[lo:lo + SUB], bh[lo:lo + SUB]
                    dec = jnp.exp2(jnp.minimum(bb - bh[lo + s:lo + s + 1, :], 0.0))
                    m = qb * dec * kh[lo + s:lo + s + 1, :]
                    blk = jnp.where(col == lo + s, jnp.sum(m, axis=-1, keepdims=True), blk)
            blocks.append(blk)
        att_diag.append(jnp.concatenate(blocks, axis=0))
        tick()

    ti = lax.broadcasted_iota(jnp.int32, (c, c), 0)
    si = lax.broadcasted_iota(jnp.int32, (c, c), 1)
    causal = si <= ti

    o_parts, s_new = [], []
    for h in range(HEADS):
        vh = v[:, h * DV:(h + 1) * DV]
        att = att_off[h] + jnp.where(causal, att_diag[h], 0.0)
        lhs = jnp.concatenate([q_dec[h], att.astype(BF16)], axis=1)
        rhs = jnp.concatenate([s_list[h].astype(BF16), vh], axis=0)
        o_parts.append(_dot(lhs, rhs))
        dcol = jnp.broadcast_to(d_last[h], (DK, DK)).T
        kt = k_dec[h].T.astype(BF16)
        s_new.append(jnp.concatenate([dcol, dcol], axis=1) * s_list[h] + _dot(kt, vh))
    return jnp.concatenate(o_parts, axis=1), s_new


def _tri_const(c):
    return jnp.asarray(np.tril(np.ones((c, c), np.float32)), BF16)


def _conv_tile(u_ref, xs_ref, hist_ref, cvs_ref, wdwb_ref, bdwb_ref, tt, tick=lambda: None):
    seg = tt // SUBLANES
    x0 = HIST * SUBLANES
    sub7 = lax.broadcasted_iota(jnp.int32, (SUBLANES, LANES), 0) == SUBLANES - 1
    grp = SUBLANES
    for cb in range(N_LB):
        cs = slice(cb * LANES, (cb + 1) * LANES)
        for s in range(SUBLANES):
            for a0 in range(0, seg, SUBLANES):
                val = u_ref[seg * s + a0:seg * s + a0 + SUBLANES, cs]
                xs_ref[cb, pl.ds(x0 + SUBLANES * a0 + s, SUBLANES, stride=SUBLANES), :] = val
        for a in range(seg - HIST, seg):
            e = a - (seg - HIST)
            new = xs_ref[cb, x0 + SUBLANES * a:x0 + SUBLANES * (a + 1), :]
            old = hist_ref[cb, SUBLANES * e:SUBLANES * (e + 1), :]
            xs_ref[cb, SUBLANES * e:SUBLANES * (e + 1), :] = pltpu.roll(jnp.where(sub7, old, new), 1, axis=0)
            hist_ref[cb, SUBLANES * e:SUBLANES * (e + 1), :] = new
        w = [wdwb_ref[SUBLANES * j:SUBLANES * (j + 1), cs] for j in range(CONV_W)]
        for g0 in range(0, seg, grp):
            if g0 % (seg // 2) == 0:
                tick()
            acc = [bdwb_ref[:, cs] for _ in range(grp)]
            for d in range(g0, g0 + grp + HIST):
                x = xs_ref[cb, SUBLANES * d:SUBLANES * (d + 1), :]
                for i in range(grp):
                    j = d - (g0 + i)
                    if 0 <= j < CONV_W:
                        acc[i] = acc[i] + w[j] * x
            for i in range(grp):
                cvs_ref[cb, pl.ds(g0 + i, SUBLANES, stride=seg), :] = acc[i]


def _split_wsq(wsq_ref):
    return wsq_ref.at[0], wsq_ref.at[1], wsq_ref.at[2], wsq_ref.at[3]


def _prompt_kernel(xn_ref, xb_ref, pb_ref, gpre_ref, wa_ref, wb_ref, wr_ref, wup_ref, bup_ref, tri_ref,
                   wdwb_ref, bdwb_ref, ggla_ref, gln_ref, bln_ref, wsq_ref, wpe_ref,
                   gfin_ref,
                   y_ref, s_ref, cs_ref,
                   q_ref, k_ref, la_ref, v_ref, h_ref, o_ref, xs_ref, hist_ref, cvs_ref, sst_ref,
                   st_yb_ref, st_ma_ref, st_sgb_ref, u_ref, sza_ref, szb_ref, mg_ref, x1_ref, x1b_ref,
                   *, tt, n_t, n_tiles):
    wao_ref, wbo_ref, wo_ref, wpg_ref = _split_wsq(wsq_ref)
    i = pl.program_id(0)
    live = i < n_tiles
    t = jnp.minimum(i, n_tiles - 1) % n_t
    cur = i % 2
    nxt = 1 - cur

    @pl.when(jnp.logical_and(t == 0, live))
    def _():
        sst_ref[...] = jnp.zeros(sst_ref.shape, F32)
        hist_ref[...] = jnp.zeros(hist_ref.shape, F32)

    @pl.when(i == 0)
    def _():
        st_yb_ref[...] = jnp.zeros(st_yb_ref.shape, BF16)
        st_ma_ref[...] = jnp.zeros(st_ma_ref.shape, F32)
        st_sgb_ref[...] = jnp.zeros(st_sgb_ref.shape, F32)
        h0 = _rms(xb_ref[0], gpre_ref[...]).astype(BF16)
        h_ref[0] = h0
        q_ref[0], k_ref[0], la_ref[0], v_ref[0] = _qkv_la(h0, wa_ref, wr_ref, wup_ref, bup_ref)

    c = D_MODEL
    za0 = 2 * KEY_DIM + VAL_DIM

    def sliced(store, fn, width=D_MODEL):
        step = min(MXU_SLICE, width)
        return [functools.partial(lambda lo: store(lo, lo + step, fn(lo, lo + step)), n * step)
                for n in range(width // step)]

    def into(ref, *lead):
        def store(lo, hi, val):
            ref[(*lead, slice(None), slice(lo, hi))] = val
        return store

    def ticker(jobs, n_ticks):
        state = [0, 0]

        def tick():
            state[0] += 1
            upto = -(-len(jobs) * state[0] // n_ticks)
            while state[1] < min(upto, len(jobs)):
                jobs[state[1]]()
                state[1] += 1
        return tick

    def store_x1(lo, hi, val):
        x1_ref[:, lo:hi] = val
        x1b_ref[:, lo:hi] = val.astype(BF16)

    def back_jobs():
        pb = pb_ref[0].astype(BF16)
        back1 = sliced(into(mg_ref), lambda lo, hi: (
            st_ma_ref[:, lo:hi] + st_sgb_ref[:, lo:hi] * _dot(st_yb_ref[...], wbo_ref[:, lo:hi])).astype(BF16))
        back2 = sliced(store_x1, lambda lo, hi: xb_ref[0, :, lo:hi] + _dot(mg_ref[...], wo_ref[:, lo:hi]))
        back3 = sliced(into(y_ref, 0), lambda lo, hi: (
            x1_ref[:, lo:hi] + _sigmoid(_dot(x1b_ref[...], wpg_ref[:, lo:hi])) * _dot(pb, wpe_ref[:, lo:hi])))
        return back1, back2, back3

    @pl.when(live)
    def _():
        h = h_ref[cur]
        hn = _rms(xn_ref[0], gpre_ref[...]).astype(BF16)
        h_ref[nxt] = hn

        def store_la(lo, hi, r):
            la_ref[nxt] = _log_sigmoid(_dot(r.astype(BF16), wup_ref[...]) + bup_ref[...]) * (LOG2E / TAU)

        ahead = (sliced(into(q_ref, nxt), lambda lo, hi: _dot(hn, wa_ref[:, lo:hi]) * (DK ** -0.5), KEY_DIM)
                 + sliced(into(k_ref, nxt), lambda lo, hi: _dot(hn, wa_ref[:, KEY_DIM + lo:KEY_DIM + hi]),
                          KEY_DIM)
                 + sliced(into(v_ref, nxt), lambda lo, hi: _dot(
                     hn, wa_ref[:, 2 * KEY_DIM + lo:2 * KEY_DIM + hi]).astype(BF16), VAL_DIM)
                 + sliced(store_la, lambda lo, hi: _dot(hn, wr_ref[...]), MXU_SLICE))

        back1, back2, back3 = back_jobs()
        n_chunks = tt // CHUNK
        jobs = (sliced(into(u_ref), lambda lo, hi: (_dot(h, wb_ref[:, lo:hi])
                                                    * _sigmoid(_dot(h, wb_ref[:, c + lo:c + hi]))))
                + back1 + sliced(into(sza_ref), lambda lo, hi: _silu(_dot(h, wa_ref[:, za0 + lo:za0 + hi])))
                + back2 + ahead)
        tick = ticker(jobs, n_chunks * HEADS)
        for ci in range(n_chunks):
            rows = slice(ci * CHUNK, (ci + 1) * CHUNK)
            s_list = [sst_ref[hd] for hd in range(HEADS)]
            o, s_new = _gla_chunk(q_ref[cur, rows, :], k_ref[cur, rows, :], la_ref[cur, rows, :],
                                  v_ref[cur, rows, :], s_list, tri_ref[...], CHUNK, tick)
            o_ref[rows, :] = o
            for hd in range(HEADS):
                sst_ref[hd] = s_new[hd]

        ona = _gated_heads(o_ref[...], sza_ref[...], ggla_ref)
        jobs = (sliced(into(st_ma_ref), lambda lo, hi: _sigmoid(_dot(h, wb_ref[:, 3 * c + lo:3 * c + hi])))
                + back3
                + sliced(into(st_ma_ref), lambda lo, hi: st_ma_ref[:, lo:hi] * _dot(ona, wao_ref[:, lo:hi]))
                + sliced(into(szb_ref), lambda lo, hi: _silu(_dot(h, wb_ref[:, 2 * c + lo:2 * c + hi])))
                + sliced(into(st_sgb_ref), lambda lo, hi: _sigmoid(_dot(h, wb_ref[:, 4 * c + lo:4 * c + hi]))))
        tick = ticker(jobs, 2 * N_LB)
        _conv_tile(u_ref, xs_ref, hist_ref, cvs_ref, wdwb_ref, bdwb_ref, tt, tick)
        y_ref[0] = _rms(y_ref[0], gfin_ref[...])

        cv = jnp.concatenate([cvs_ref[cb] for cb in range(N_LB)], axis=1)
        st_yb_ref[...] = _conv_branch_in(cv, szb_ref[...], gln_ref, bln_ref)
        s_ref[0] = sst_ref[...]
        cs_ref[0] = u_ref[tt - HIST:tt, :]

    @pl.when(jnp.logical_not(live))
    def _():
        for job in sum(back_jobs(), []):
            job()
        y_ref[0] = _rms(y_ref[0], gfin_ref[...])


def _whole():
    return pl.BlockSpec(memory_space=pltpu.VMEM)


def _prompt_call(x, p, w, tt):
    bsz, t_len, _ = x.shape
    seg = tt // SUBLANES
    assert t_len % tt == 0 and tt % CHUNK == 0 and seg >= HIST and seg % SUBLANES == 0
    n_t = t_len // tt
    n_tiles = bsz * n_t
    front = lambda i: jnp.minimum(i, n_tiles - 1)
    back = lambda i: jnp.maximum(i - 1, 0)
    ahead = lambda i: jnp.minimum(i + 1, n_tiles - 1)
    tok = lambda wd, tile: pl.BlockSpec((1, tt, wd), lambda i: (tile(i) // n_t, tile(i) % n_t, 0))
    names = ["gpre", "wa", "wb", "wr", "wup", "bup", "tri", "wdwb", "bdwb", "ggla", "gln", "bln", "wsq",
             "wpe", "gfin"]
    return pl.pallas_call(
        functools.partial(_prompt_kernel, tt=tt, n_t=n_t, n_tiles=n_tiles),
        grid=(n_tiles + 1,),
        in_specs=[tok(D_MODEL, ahead), tok(D_MODEL, back), tok(PLE, back)] + [_whole()] * len(names),
        out_specs=[tok(D_MODEL, back),
                   pl.BlockSpec((1, HEADS, DK, DV), lambda i: (front(i) // n_t, 0, 0, 0)),
                   pl.BlockSpec((1, HIST, D_MODEL), lambda i: (front(i) // n_t, 0, 0))],
        out_shape=[jax.ShapeDtypeStruct((bsz, t_len, D_MODEL), F32),
                   jax.ShapeDtypeStruct((bsz, HEADS, DK, DV), F32),
                   jax.ShapeDtypeStruct((bsz, HIST, D_MODEL), F32)],
        scratch_shapes=[pltpu.VMEM((2, tt, KEY_DIM), F32), pltpu.VMEM((2, tt, KEY_DIM), F32),
                        pltpu.VMEM((2, tt, KEY_DIM), F32), pltpu.VMEM((2, tt, VAL_DIM), BF16),
                        pltpu.VMEM((2, tt, D_MODEL), BF16),
                        pltpu.VMEM((tt, VAL_DIM), F32),
                        pltpu.VMEM((N_LB, (HIST + seg) * SUBLANES, LANES), F32),
                        pltpu.VMEM((N_LB, HIST * SUBLANES, LANES), F32),
                        pltpu.VMEM((N_LB, tt, LANES), F32),
                        pltpu.VMEM((HEADS, DK, DV), F32),
                        pltpu.VMEM((tt, D_MODEL), BF16),
                        pltpu.VMEM((tt, D_MODEL), F32), pltpu.VMEM((tt, D_MODEL), F32),
                        pltpu.VMEM((tt, D_MODEL), F32), pltpu.VMEM((tt, D_MODEL), F32),
                        pltpu.VMEM((tt, D_MODEL), F32), pltpu.VMEM((tt, D_MODEL), BF16),
                        pltpu.VMEM((tt, D_MODEL), F32), pltpu.VMEM((tt, D_MODEL), BF16)],
        compiler_params=pltpu.CompilerParams(dimension_semantics=("arbitrary",),
                                             vmem_limit_bytes=VMEM_LIMIT),
        name="prompt_layer",
    )(x, x, p, *[w[n] for n in names])


def _pre_kernel(x_ref, gpre_ref, wa_ref, wb_ref, wr_ref, wup_ref, bup_ref,
                q_ref, k_ref, la_ref, v_ref, u_ref):
    h = _rms(x_ref[...], gpre_ref[...]).astype(BF16)
    q_ref[...], k_ref[...], la_ref[...], v_ref[...] = _qkv_la(h, wa_ref, wr_ref, wup_ref, bup_ref)
    u_ref[...] = _glu(h, wb_ref)


def _pre_call(x2d, w, tm):
    n = x2d.shape[0]
    row = lambda wd: pl.BlockSpec((tm, wd), lambda i: (i, 0))
    outs = [(KEY_DIM, F32), (KEY_DIM, F32), (KEY_DIM, F32), (VAL_DIM, BF16), (D_MODEL, F32)]
    names = ["gpre", "wa", "wb", "wr", "wup", "bup"]
    return pl.pallas_call(
        _pre_kernel,
        grid=(n // tm,),
        in_specs=[row(D_MODEL)] + [_whole()] * len(names),
        out_specs=[row(wd) for wd, _ in outs],
        out_shape=[jax.ShapeDtypeStruct((n, wd), dt) for wd, dt in outs],
        compiler_params=pltpu.CompilerParams(dimension_semantics=("arbitrary",),
                                             vmem_limit_bytes=VMEM_LIMIT),
        name="sample_pre",
    )(x2d, *[w[n_] for n_ in names])


def _sample_conv_kernel(c0_ref, u_ref, wdwb_ref, bdwb_ref, cv_ref, cs_ref, *, t_len, nbs):
    ext = lambda d, rows, cs: c0_ref[d, rows, cs] if d < HIST else u_ref[d - HIST, rows, cs]
    for r0 in range(0, nbs, SUBLANES):
        rows = slice(r0, r0 + SUBLANES)
        for cb in range(N_LB):
            cs = slice(cb * LANES, (cb + 1) * LANES)
            acc = [bdwb_ref[:, cs] for _ in range(t_len)]
            for d in range(HIST + t_len):
                x = ext(d, rows, cs)
                for t in range(t_len):
                    j = d - t
                    if 0 <= j < CONV_W:
                        acc[t] = acc[t] + wdwb_ref[SUBLANES * j:SUBLANES * (j + 1), cs] * x
            for t in range(t_len):
                cv_ref[t, rows, cs] = acc[t]
    for d in range(HIST):
        cs_ref[d] = ext(d + t_len, slice(None), slice(None))


def _sample_conv_call(c0t, ut, w, nbs):
    _, bsz, _ = c0t.shape
    t_len = ut.shape[0]
    assert bsz % nbs == 0 and nbs % SUBLANES == 0
    blk = lambda rows: pl.BlockSpec((rows, nbs, D_MODEL), lambda i: (0, i, 0))
    return pl.pallas_call(
        functools.partial(_sample_conv_kernel, t_len=t_len, nbs=nbs),
        grid=(bsz // nbs,),
        in_specs=[blk(HIST), blk(t_len), _whole(), _whole()],
        out_specs=[blk(t_len), blk(HIST)],
        out_shape=[jax.ShapeDtypeStruct((t_len, bsz, D_MODEL), F32),
                   jax.ShapeDtypeStruct((HIST, bsz, D_MODEL), F32)],
        compiler_params=pltpu.CompilerParams(dimension_semantics=("arbitrary",),
                                             vmem_limit_bytes=VMEM_LIMIT),
        name="sample_conv",
    )(c0t, ut, w["wdwb"], w["bdwb"])


def _mix_sample_kernel(q_ref, k_ref, la_ref, v_ref, s0_ref, tri_ref, o_ref, s_ref, *, nb, t_len):
    pad = PACK_ROWS - t_len

    def padf(z):
        zf = z.astype(F32)
        return jnp.concatenate([zf, jnp.zeros((pad, z.shape[1]), F32)], axis=0).astype(z.dtype)

    def pair(m, carry):
        for slot in range(SEQ_UNROLL):
            n = SEQ_UNROLL * m + slot
            s_list = [s0_ref[n, hd] for hd in range(HEADS)]
            o, s_new = _gla_chunk(padf(q_ref[n]), padf(k_ref[n]), padf(la_ref[n]), padf(v_ref[n]),
                                  s_list, tri_ref[...], t_len)
            o_ref[n] = o[0:t_len]
            for hd in range(HEADS):
                s_ref[n, hd] = s_new[hd]
        return carry

    lax.fori_loop(0, nb // SEQ_UNROLL, pair, 0)


def _mix_sample_call(q, k, la, v, s0, nb):
    bsz, t_len, _ = q.shape
    assert bsz % nb == 0 and nb % SEQ_UNROLL == 0 and t_len <= SUB
    tok = lambda wd: pl.BlockSpec((nb, t_len, wd), lambda i: (i, 0, 0))
    st = pl.BlockSpec((nb, HEADS, DK, DV), lambda i: (i, 0, 0, 0))
    return pl.pallas_call(
        functools.partial(_mix_sample_kernel, nb=nb, t_len=t_len),
        grid=(bsz // nb,),
        in_specs=[tok(KEY_DIM), tok(KEY_DIM), tok(KEY_DIM), tok(VAL_DIM), st, _whole()],
        out_specs=[tok(VAL_DIM), st],
        out_shape=[jax.ShapeDtypeStruct((bsz, t_len, VAL_DIM), F32),
                   jax.ShapeDtypeStruct((bsz, HEADS, DK, DV), F32)],
        compiler_params=pltpu.CompilerParams(dimension_semantics=("arbitrary",),
                                             vmem_limit_bytes=VMEM_LIMIT),
        name="sample_mix",
    )(q, k, la, v, s0, _tri_const(PACK_ROWS))


def _post_kernel(x_ref, o_ref, cv_ref, p_ref, gpre_ref, wa_ref, wb_ref,
                 ggla_ref, gln_ref, bln_ref, wsq_ref, wpe_ref, gfin_ref, y_ref):
    wao_ref, wbo_ref, wo_ref, wpg_ref = _split_wsq(wsq_ref)
    x = x_ref[...]
    c = D_MODEL
    h = _rms(x, gpre_ref[...]).astype(BF16)
    y_a = _dot(_gated_heads(o_ref[...], _z_a(h, wa_ref), ggla_ref), wao_ref[...])
    szb = _silu(_dot(h, wb_ref[:, 2 * c:3 * c]))
    sga = _sigmoid(_dot(h, wb_ref[:, 3 * c:4 * c]))
    sgb = _sigmoid(_dot(h, wb_ref[:, 4 * c:5 * c]))
    pe = _dot(p_ref[...].astype(BF16), wpe_ref[...])
    y_ref[...] = _tail(x, y_a, cv_ref[...], szb, sga, sgb, pe, gln_ref, bln_ref, wbo_ref, wo_ref, wpg_ref, gfin_ref)


def _post_call(x2d, o, cv, p2d, w, tm):
    n = x2d.shape[0]
    row = lambda wd: pl.BlockSpec((tm, wd), lambda i: (i, 0))
    names = ["gpre", "wa", "wb", "ggla", "gln", "bln", "wsq", "wpe", "gfin"]
    return pl.pallas_call(
        _post_kernel,
        grid=(n // tm,),
        in_specs=[row(D_MODEL)] * 3 + [row(PLE)] + [_whole()] * len(names),
        out_specs=row(D_MODEL),
        out_shape=jax.ShapeDtypeStruct((n, D_MODEL), F32),
        compiler_params=pltpu.CompilerParams(dimension_semantics=("arbitrary",),
                                             vmem_limit_bytes=VMEM_LIMIT),
        name="sample_post",
    )(x2d, o, cv, p2d, *[w[n_] for n_ in names])


def _sample_layer(x, p, s0, c0, w, tm, nb, nbs):
    bsz, t_len, _ = x.shape
    x2d = x.reshape(bsz * t_len, D_MODEL)
    q, k, la, v, u = _pre_call(x2d, w, tm)
    r3 = lambda z: z.reshape(bsz, t_len, z.shape[-1])
    tmaj = lambda z: jnp.transpose(z, (1, 0, 2))
    o, s_new = _mix_sample_call(r3(q), r3(k), r3(la), r3(v), s0, nb)
    cv_t, c_new_t = _sample_conv_call(tmaj(c0), tmaj(r3(u)), w, nbs)
    y = _post_call(x2d, o.reshape(-1, VAL_DIM), tmaj(cv_t).reshape(-1, D_MODEL), p.reshape(bsz * t_len, PLE), w, tm)
    return y.reshape(bsz, t_len, D_MODEL), s_new, tmaj(c_new_t)


def kernel(x_prompt, x_sample, state_gla, state_conv, p_prompt, p_sample, g_pre, w_in, w_a_up, b_a_up, g_gla, w_a_out, w_dw, b_dw, g_ln, b_ln, w_b_out, w_o, w_pe, w_pg, g_final):
    depth = w_in.shape[0]
    assert depth == 1, "one trunk layer"
    i = 0
    tl = _tiles()
    n_a = 2 * KEY_DIM + 2 * VAL_DIM
    row = lambda z: z.reshape(1, -1)
    w = {
        "gpre": row(g_pre[i]),
        "wa": w_in[i][:, :n_a].astype(BF16),
        "wb": w_in[i][:, n_a + RANK:].astype(BF16),
        "wr": jnp.pad(w_in[i][:, n_a:n_a + RANK], ((0, 0), (0, LANES - RANK))).astype(BF16),
        "wup": jnp.pad(w_a_up[i], ((0, LANES - RANK), (0, 0))).astype(BF16),
        "bup": row(b_a_up[i]),
        "tri": _tri_const(CHUNK),
        "wdwb": jnp.repeat(w_dw[i], SUBLANES, axis=0),
        "bdwb": jnp.broadcast_to(row(b_dw[i]), (SUBLANES, D_MODEL)),
        "ggla": row(g_gla[i]), "gln": row(g_ln[i]), "bln": row(b_ln[i]),
        "wsq": jnp.stack([w_a_out[i], w_b_out[i], w_o[i], w_pg[i]]).astype(BF16),
        "wpe": w_pe[i].astype(BF16), "gfin": row(g_final),
    }
    yp, sp, cp = _prompt_call(x_prompt, p_prompt[i], w, tl["tt"])
    ys, ss, cs = _sample_layer(x_sample, p_sample[i], state_gla[i], state_conv[i], w, tl["tm"], tl["nb"],
                               tl["nbs"])
    return (yp, ys, sp[None], cp[None], ss[None], cs[None])
```

```python
import functools

import numpy as np
import jax
import jax.numpy as jnp
from jax import lax
from jax.experimental import pallas as pl
from jax.experimental.pallas import tpu as pltpu

F32 = jnp.float32
BF16 = jnp.bfloat16

D_MODEL = 1024
HEADS = 4
DK = 128
DV = 256
KEY_DIM = HEADS * DK
VAL_DIM = HEADS * DV
RANK = 16
TAU = 16.0
CHUNK = 64
SUB = 8
PACK_ROWS = 16
CONV_W = 31
HIST = CONV_W - 1
PLE = 256
EPS = 1e-6
LOG2E = 1.4426950408889634
LANES = 128
SUBLANES = 8
N_LB = D_MODEL // LANES
MXU_SLICE = 1024
SEQ_UNROLL = 4
VMEM_LIMIT = 62 * 1024 * 1024


def _tiles():
    return dict(tt=256, tm=256, nb=8, nbs=16)


def _dot(a, b):
    return jnp.dot(a, b, preferred_element_type=F32)


def _sigmoid(x):
    return 1.0 / (1.0 + jnp.exp2(x * (-LOG2E)))


def _silu(x):
    return x * _sigmoid(x)


def _log_sigmoid(x):
    return -(jnp.maximum(-x, 0.0) + jnp.log(1.0 + jnp.exp(-jnp.abs(x))))


def _rms(x, g):
    return x * lax.rsqrt(jnp.mean(x * x, axis=-1, keepdims=True) + EPS) * g


def _qkv_la(h, wa_ref, wr_ref, wup_ref, bup_ref):
    q = _dot(h, wa_ref[:, 0:KEY_DIM]) * (DK ** -0.5)
    k = _dot(h, wa_ref[:, KEY_DIM:2 * KEY_DIM])
    v = _dot(h, wa_ref[:, 2 * KEY_DIM:2 * KEY_DIM + VAL_DIM]).astype(BF16)
    r = _dot(h, wr_ref[...]).astype(BF16)
    la = _log_sigmoid(_dot(r, wup_ref[...]) + bup_ref[...]) * (LOG2E / TAU)
    return q, k, la, v


def _z_a(h, wa_ref):
    return _silu(_dot(h, wa_ref[:, 2 * KEY_DIM + VAL_DIM:]))


def _glu(h, wb_ref):
    c = D_MODEL
    return _dot(h, wb_ref[:, 0:c]) * _sigmoid(_dot(h, wb_ref[:, c:2 * c]))


def _gated_heads(o, sza, ggla_ref):
    g = ggla_ref[...]
    on = jnp.concatenate([_rms(o[:, hd * DV:(hd + 1) * DV], g) for hd in range(HEADS)], axis=1)
    return (on * sza).astype(BF16)


def _conv_branch_in(cv, szb, gln_ref, bln_ref):
    mu = jnp.mean(cv, axis=-1, keepdims=True)
    xc = cv - mu
    ln = xc * lax.rsqrt(jnp.mean(xc * xc, axis=-1, keepdims=True) + EPS) * gln_ref[...] + bln_ref[...]
    return (_silu(ln) * szb).astype(BF16)


def _tail(x, y_a, cv, szb, sga, sgb, pe, gln_ref, bln_ref, wbo_ref, wo_ref, wpg_ref, gfin_ref):
    y_b = _dot(_conv_branch_in(cv, szb, gln_ref, bln_ref), wbo_ref[...])
    merged = sga * y_a + sgb * y_b
    x1 = x + _dot(merged.astype(BF16), wo_ref[...])
    x2 = x1 + _sigmoid(_dot(x1.astype(BF16), wpg_ref[...])) * pe
    return _rms(x2, gfin_ref[...])


def _gla_chunk(q, k, a, v, s_list, tri, n_keys, tick=lambda: None):
    c = q.shape[0]
    n_sub = c // SUB
    k_used = min(SUB, n_keys)
    a0 = a.astype(BF16)
    r1 = a - a0.astype(F32)
    a1 = r1.astype(BF16)
    a2 = (r1 - a1.astype(F32)).astype(BF16)
    b = _dot(tri, a0) + _dot(tri, a1) + _dot(tri, a2)

    att_off, att_diag, q_dec, k_dec, d_last = [], [], [], [], []
    for h in range(HEADS):
        sl = slice(h * DK, (h + 1) * DK)
        qh, kh, bh = q[:, sl], k[:, sl], b[:, sl]
        bl = bh[c - 1:c, :]
        q_dec.append((qh * jnp.exp2(bh)).astype(BF16))
        k_dec.append(kh * jnp.exp2(bl - bh))
        d_last.append(jnp.exp2(bl))
        zrow = jnp.zeros((SUB, DK), F32)
        rows = [jnp.zeros((SUB, c), F32)]
        for i in range(1, n_sub):
            lo = i * SUB
            if lo >= n_keys:
                rows.append(jnp.zeros((SUB, c), F32))
                continue
            br = bh[lo - 1:lo, :]
            qi = jnp.concatenate([qh[lo:lo + SUB] * jnp.exp2(bh[lo:lo + SUB] - br), zrow], axis=0).astype(BF16)
            ki = kh[:lo] * jnp.exp2(br - bh[:lo])
            kfull = jnp.concatenate([ki, jnp.zeros((c - lo, DK), F32)], axis=0).astype(BF16)
            rows.append(lax.dot_general(qi, kfull, (((1,), (1,)), ((), ())),
                                        preferred_element_type=F32)[0:SUB])
        att_off.append(jnp.concatenate(rows, axis=0))
        col = lax.broadcasted_iota(jnp.int32, (SUB, c), 1)
        blocks = []
        for lo in range(0, c, SUB):
            blk = jnp.zeros((SUB, c), F32)
            for s in range(k_used):
                if lo + s < n_keys:
                    qb, bb = qh[lo:lo + SUB], bh[lo:lo + SUB]
                    dec = jnp.exp2(jnp.minimum(bb - bh[lo + s:lo + s + 1, :], 0.0))
                    m = qb * dec * kh[lo + s:lo + s + 1, :]
                    blk = jnp.where(col == lo + s, jnp.sum(m, axis=-1, keepdims=True), blk)
            blocks.append(blk)
        att_diag.append(jnp.concatenate(blocks, axis=0))
        tick()

    ti = lax.broadcasted_iota(jnp.int32, (c, c), 0)
    si = lax.broadcasted_iota(jnp.int32, (c, c), 1)
    causal = si <= ti

    o_parts, s_new = [], []
    for h in range(HEADS):
        vh = v[:, h * DV:(h + 1) * DV]
        att = att_off[h] + jnp.where(causal, att_diag[h], 0.0)
        lhs = jnp.concatenate([q_dec[h], att.astype(BF16)], axis=1)
        rhs = jnp.concatenate([s_list[h].astype(BF16), vh], axis=0)
        o_parts.append(_dot(lhs, rhs))
        dcol = jnp.broadcast_to(d_last[h], (DK, DK)).T
        kt = k_dec[h].T.astype(BF16)
        s_new.append(jnp.concatenate([dcol, dcol], axis=1) * s_list[h] + _dot(kt, vh))
    return jnp.concatenate(o_parts, axis=1), s_new


def _tri_const(c):
    return jnp.asarray(np.tril(np.ones((c, c), np.float32)), BF16)


def _conv_tile(u_ref, xs_ref, hist_ref, cvs_ref, wdwb_ref, bdwb_ref, tt, tick=lambda: None):
    seg = tt // SUBLANES
    x0 = HIST * SUBLANES
    sub7 = lax.broadcasted_iota(jnp.int32, (SUBLANES, LANES), 0) == SUBLANES - 1
    grp = SUBLANES
    for cb in range(N_LB):
        cs = slice(cb * LANES, (cb + 1) * LANES)
        for s in range(SUBLANES):
            for a0 in range(0, seg, SUBLANES):
                val = u_ref[seg * s + a0:seg * s + a0 + SUBLANES, cs]
                xs_ref[cb, pl.ds(x0 + SUBLANES * a0 + s, SUBLANES, stride=SUBLANES), :] = val
        for a in range(seg - HIST, seg):
            e = a - (seg - HIST)
            new = xs_ref[cb, x0 + SUBLANES * a:x0 + SUBLANES * (a + 1), :]
            old = hist_ref[cb, SUBLANES * e:SUBLANES * (e + 1), :]
            xs_ref[cb, SUBLANES * e:SUBLANES * (e + 1), :] = pltpu.roll(jnp.where(sub7, old, new), 1, axis=0)
            hist_ref[cb, SUBLANES * e:SUBLANES * (e + 1), :] = new
        w = [wdwb_ref[SUBLANES * j:SUBLANES * (j + 1), cs] for j in range(CONV_W)]
        for g0 in range(0, seg, grp):
            if g0 % (seg // 2) == 0:
                tick()
            acc = [bdwb_ref[:, cs] for _ in range(grp)]
            for d in range(g0, g0 + grp + HIST):
                x = xs_ref[cb, SUBLANES * d:SUBLANES * (d + 1), :]
                for i in range(grp):
                    j = d - (g0 + i)
                    if 0 <= j < CONV_W:
                        acc[i] = acc[i] + w[j] * x
            for i in range(grp):
                cvs_ref[cb, pl.ds(g0 + i, SUBLANES, stride=seg), :] = acc[i]


def _split_wsq(wsq_ref):
    return wsq_ref.at[0], wsq_ref.at[1], wsq_ref.at[2], wsq_ref.at[3]


def _prompt_kernel(xn_ref, xb_ref, pb_ref, gpre_ref, wa_ref, wb_ref, wr_ref, wup_ref, bup_ref, tri_ref,
                   wdwb_ref, bdwb_ref, ggla_ref, gln_ref, bln_ref, wsq_ref, wpe_ref,
                   gfin_ref,
                   y_ref, s_ref, cs_ref,
                   q_ref, k_ref, la_ref, v_ref, h_ref, o_ref, xs_ref, hist_ref, cvs_ref, sst_ref,
                   st_yb_ref, st_ma_ref, st_sgb_ref, u_ref, sza_ref, szb_ref, mg_ref, x1_ref, x1b_ref,
                   *, tt, n_t, n_tiles):
    wao_ref, wbo_ref, wo_ref, wpg_ref = _split_wsq(wsq_ref)
    i = pl.program_id(0)
    live = i < n_tiles
    t = jnp.minimum(i, n_tiles - 1) % n_t
    cur = i % 2
    nxt = 1 - cur

    @pl.when(jnp.logical_and(t == 0, live))
    def _():
        sst_ref[...] = jnp.zeros(sst_ref.shape, F32)
        hist_ref[...] = jnp.zeros(hist_ref.shape, F32)

    @pl.when(i == 0)
    def _():
        st_yb_ref[...] = jnp.zeros(st_yb_ref.shape, BF16)
        st_ma_ref[...] = jnp.zeros(st_ma_ref.shape, F32)
        st_sgb_ref[...] = jnp.zeros(st_sgb_ref.shape, F32)
        h0 = _rms(xb_ref[0], gpre_ref[...]).astype(BF16)
        h_ref[0] = h0
        q_ref[0], k_ref[0], la_ref[0], v_ref[0] = _qkv_la(h0, wa_ref, wr_ref, wup_ref, bup_ref)

    c = D_MODEL
    za0 = 2 * KEY_DIM + VAL_DIM

    def sliced(store, fn, width=D_MODEL):
        step = min(MXU_SLICE, width)
        return [functools.partial(lambda lo: store(lo, lo + step, fn(lo, lo + step)), n * step)
                for n in range(width // step)]

    def into(ref, *lead):
        def store(lo, hi, val):
            ref[(*lead, slice(None), slice(lo, hi))] = val
        return store

    def ticker(jobs, n_ticks):
        state = [0, 0]

        def tick():
            state[0] += 1
            upto = -(-len(jobs) * state[0] // n_ticks)
            while state[1] < min(upto, len(jobs)):
                jobs[state[1]]()
                state[1] += 1
        return tick

    def store_x1(lo, hi, val):
        x1_ref[:, lo:hi] = val
        x1b_ref[:, lo:hi] = val.astype(BF16)

    def back_jobs():
        pb = pb_ref[0].astype(BF16)
        back1 = sliced(into(mg_ref), lambda lo, hi: (
            st_ma_ref[:, lo:hi] + st_sgb_ref[:, lo:hi] * _dot(st_yb_ref[...], wbo_ref[:, lo:hi])).astype(BF16))
        back2 = sliced(store_x1, lambda lo, hi: xb_ref[0, :, lo:hi] + _dot(mg_ref[...], wo_ref[:, lo:hi]))
        back3 = sliced(into(y_ref, 0), lambda lo, hi: (
            x1_ref[:, lo:hi] + _sigmoid(_dot(x1b_ref[...], wpg_ref[:, lo:hi])) * _dot(pb, wpe_ref[:, lo:hi])))
        return back1, back2, back3

    @pl.when(live)
    def _():
        h = h_ref[cur]
        hn = _rms(xn_ref[0], gpre_ref[...]).astype(BF16)
        h_ref[nxt] = hn

        def store_la(lo, hi, r):
            la_ref[nxt] = _log_sigmoid(_dot(r.astype(BF16), wup_ref[...]) + bup_ref[...]) * (LOG2E / TAU)

        def job_qkv():
            r = _dot(hn, wa_ref[:, 0:2 * KEY_DIM + VAL_DIM])
            q_ref[nxt] = r[:, 0:KEY_DIM] * (DK ** -0.5)
            k_ref[nxt] = r[:, KEY_DIM:2 * KEY_DIM]
            v_ref[nxt] = r[:, 2 * KEY_DIM:].astype(BF16)

        def job_u():
            r = _dot(h, wb_ref[:, 0:2 * c])
            u_ref[...] = r[:, 0:c] * _sigmoid(r[:, c:])

        ahead = [job_qkv] + sliced(store_la, lambda lo, hi: _dot(hn, wr_ref[...]), MXU_SLICE)

        back1, back2, back3 = back_jobs()
        n_chunks = tt // CHUNK
        jobs = ([job_u]
                + back1 + sliced(into(sza_ref), lambda lo, hi: _silu(_dot(h, wa_ref[:, za0 + lo:za0 + hi])))
                + back2 + ahead)
        tick = ticker(jobs, n_chunks * HEADS)
        for ci in range(n_chunks):
            rows = slice(ci * CHUNK, (ci + 1) * CHUNK)
            s_list = [sst_ref[hd] for hd in range(HEADS)]
            o, s_new = _gla_chunk(q_ref[cur, rows, :], k_ref[cur, rows, :], la_ref[cur, rows, :],
                                  v_ref[cur, rows, :], s_list, tri_ref[...], CHUNK, tick)
            o_ref[rows, :] = o
            for hd in range(HEADS):
                sst_ref[hd] = s_new[hd]

        ona = _gated_heads(o_ref[...], sza_ref[...], ggla_ref)
        jobs = (sliced(into(st_ma_ref), lambda lo, hi: _sigmoid(_dot(h, wb_ref[:, 3 * c + lo:3 * c + hi])))
                + back3
                + sliced(into(st_ma_ref), lambda lo, hi: st_ma_ref[:, lo:hi] * _dot(ona, wao_ref[:, lo:hi]))
                + sliced(into(szb_ref), lambda lo, hi: _silu(_dot(h, wb_ref[:, 2 * c + lo:2 * c + hi])))
                + sliced(into(st_sgb_ref), lambda lo, hi: _sigmoid(_dot(h, wb_ref[:, 4 * c + lo:4 * c + hi]))))
        tick = ticker(jobs, 2 * N_LB)
        _conv_tile(u_ref, xs_ref, hist_ref, cvs_ref, wdwb_ref, bdwb_ref, tt, tick)
        y_ref[0] = _rms(y_ref[0], gfin_ref[...])

        cv = jnp.concatenate([cvs_ref[cb] for cb in range(N_LB)], axis=1)
        st_yb_ref[...] = _conv_branch_in(cv, szb_ref[...], gln_ref, bln_ref)
        s_ref[0] = sst_ref[...]
        cs_ref[0] = u_ref[tt - HIST:tt, :]

    @pl.when(jnp.logical_not(live))
    def _():
        for job in sum(back_jobs(), []):
            job()
        y_ref[0] = _rms(y_ref[0], gfin_ref[...])


def _whole():
    return pl.BlockSpec(memory_space=pltpu.VMEM)


def _prompt_call(x, p, w, tt):
    bsz, t_len, _ = x.shape
    seg = tt // SUBLANES
    assert t_len % tt == 0 and tt % CHUNK == 0 and seg >= HIST and seg % SUBLANES == 0
    n_t = t_len // tt
    n_tiles = bsz * n_t
    front = lambda i: jnp.minimum(i, n_tiles - 1)
    back = lambda i: jnp.maximum(i - 1, 0)
    ahead = lambda i: jnp.minimum(i + 1, n_tiles - 1)
    tok = lambda wd, tile: pl.BlockSpec((1, tt, wd), lambda i: (tile(i) // n_t, tile(i) % n_t, 0))
    names = ["gpre", "wa", "wb", "wr", "wup", "bup", "tri", "wdwb", "bdwb", "ggla", "gln", "bln", "wsq",
             "wpe", "gfin"]
    return pl.pallas_call(
        functools.partial(_prompt_kernel, tt=tt, n_t=n_t, n_tiles=n_tiles),
        grid=(n_tiles + 1,),
        in_specs=[tok(D_MODEL, ahead), tok(D_MODEL, back), tok(PLE, back)] + [_whole()] * len(names),
        out_specs=[tok(D_MODEL, back),
                   pl.BlockSpec((1, HEADS, DK, DV), lambda i: (front(i) // n_t, 0, 0, 0)),
                   pl.BlockSpec((1, HIST, D_MODEL), lambda i: (front(i) // n_t, 0, 0))],
        out_shape=[jax.ShapeDtypeStruct((bsz, t_len, D_MODEL), F32),
                   jax.ShapeDtypeStruct((bsz, HEADS, DK, DV), F32),
                   jax.ShapeDtypeStruct((bsz, HIST, D_MODEL), F32)],
        scratch_shapes=[pltpu.VMEM((2, tt, KEY_DIM), F32), pltpu.VMEM((2, tt, KEY_DIM), F32),
                        pltpu.VMEM((2, tt, KEY_DIM), F32), pltpu.VMEM((2, tt, VAL_DIM), BF16),
                        pltpu.VMEM((2, tt, D_MODEL), BF16),
                        pltpu.VMEM((tt, VAL_DIM), F32),
                        pltpu.VMEM((N_LB, (HIST + seg) * SUBLANES, LANES), F32),
                        pltpu.VMEM((N_LB, HIST * SUBLANES, LANES), F32),
                        pltpu.VMEM((N_LB, tt, LANES), F32),
                        pltpu.VMEM((HEADS, DK, DV), F32),
                        pltpu.VMEM((tt, D_MODEL), BF16),
                        pltpu.VMEM((tt, D_MODEL), F32), pltpu.VMEM((tt, D_MODEL), F32),
                        pltpu.VMEM((tt, D_MODEL), F32), pltpu.VMEM((tt, D_MODEL), F32),
                        pltpu.VMEM((tt, D_MODEL), F32), pltpu.VMEM((tt, D_MODEL), BF16),
                        pltpu.VMEM((tt, D_MODEL), F32), pltpu.VMEM((tt, D_MODEL), BF16)],
        compiler_params=pltpu.CompilerParams(dimension_semantics=("arbitrary",),
                                             vmem_limit_bytes=VMEM_LIMIT),
        name="prompt_layer",
    )(x, x, p, *[w[n] for n in names])


def _pre_kernel(x_ref, gpre_ref, wa_ref, wb_ref, wr_ref, wup_ref, bup_ref,
                q_ref, k_ref, la_ref, v_ref, u_ref):
    h = _rms(x_ref[...], gpre_ref[...]).astype(BF16)
    q_ref[...], k_ref[...], la_ref[...], v_ref[...] = _qkv_la(h, wa_ref, wr_ref, wup_ref, bup_ref)
    u_ref[...] = _glu(h, wb_ref)


def _pre_call(x2d, w, tm):
    n = x2d.shape[0]
    row = lambda wd: pl.BlockSpec((tm, wd), lambda i: (i, 0))
    outs = [(KEY_DIM, F32), (KEY_DIM, F32), (KEY_DIM, F32), (VAL_DIM, BF16), (D_MODEL, F32)]
    names = ["gpre", "wa", "wb", "wr", "wup", "bup"]
    return pl.pallas_call(
        _pre_kernel,
        grid=(n // tm,),
        in_specs=[row(D_MODEL)] + [_whole()] * len(names),
        out_specs=[row(wd) for wd, _ in outs],
        out_shape=[jax.ShapeDtypeStruct((n, wd), dt) for wd, dt in outs],
        compiler_params=pltpu.CompilerParams(dimension_semantics=("arbitrary",),
                                             vmem_limit_bytes=VMEM_LIMIT),
        name="sample_pre",
    )(x2d, *[w[n_] for n_ in names])


def _sample_conv_kernel(c0_ref, u_ref, wdwb_ref, bdwb_ref, cv_ref, cs_ref, *, t_len, nbs):
    ext = lambda d, rows, cs: c0_ref[d, rows, cs] if d < HIST else u_ref[d - HIST, rows, cs]
    for r0 in range(0, nbs, SUBLANES):
        rows = slice(r0, r0 + SUBLANES)
        for cb in range(N_LB):
            cs = slice(cb * LANES, (cb + 1) * LANES)
            acc = [bdwb_ref[:, cs] for _ in range(t_len)]
            for d in range(HIST + t_len):
                x = ext(d, rows, cs)
                for t in range(t_len):
                    j = d - t
                    if 0 <= j < CONV_W:
                        acc[t] = acc[t] + wdwb_ref[SUBLANES * j:SUBLANES * (j + 1), cs] * x
            for t in range(t_len):
                cv_ref[t, rows, cs] = acc[t]
    for d in range(HIST):
        cs_ref[d] = ext(d + t_len, slice(None), slice(None))


def _sample_conv_call(c0t, ut, w, nbs):
    _, bsz, _ = c0t.shape
    t_len = ut.shape[0]
    assert bsz % nbs == 0 and nbs % SUBLANES == 0
    blk = lambda rows: pl.BlockSpec((rows, nbs, D_MODEL), lambda i: (0, i, 0))
    return pl.pallas_call(
        functools.partial(_sample_conv_kernel, t_len=t_len, nbs=nbs),
        grid=(bsz // nbs,),
        in_specs=[blk(HIST), blk(t_len), _whole(), _whole()],
        out_specs=[blk(t_len), blk(HIST)],
        out_shape=[jax.ShapeDtypeStruct((t_len, bsz, D_MODEL), F32),
                   jax.ShapeDtypeStruct((HIST, bsz, D_MODEL), F32)],
        compiler_params=pltpu.CompilerParams(dimension_semantics=("arbitrary",),
                                             vmem_limit_bytes=VMEM_LIMIT),
        name="sample_conv",
    )(c0t, ut, w["wdwb"], w["bdwb"])


def _mix_sample_kernel(q_ref, k_ref, la_ref, v_ref, s0_ref, tri_ref, o_ref, s_ref, *, nb, t_len):
    pad = PACK_ROWS - t_len

    def padf(z):
        zf = z.astype(F32)
        return jnp.concatenate([zf, jnp.zeros((pad, z.shape[1]), F32)], axis=0).astype(z.dtype)

    def pair(m, carry):
        for slot in range(SEQ_UNROLL):
            n = SEQ_UNROLL * m + slot
            s_list = [s0_ref[n, hd] for hd in range(HEADS)]
            o, s_new = _gla_chunk(padf(q_ref[n]), padf(k_ref[n]), padf(la_ref[n]), padf(v_ref[n]),
                                  s_list, tri_ref[...], t_len)
            o_ref[n] = o[0:t_len]
            for hd in range(HEADS):
                s_ref[n, hd] = s_new[hd]
        return carry

    lax.fori_loop(0, nb // SEQ_UNROLL, pair, 0)


def _mix_sample_call(q, k, la, v, s0, nb):
    bsz, t_len, _ = q.shape
    assert bsz % nb == 0 and nb % SEQ_UNROLL == 0 and t_len <= SUB
    tok = lambda wd: pl.BlockSpec((nb, t_len, wd), lambda i: (i, 0, 0))
    st = pl.BlockSpec((nb, HEADS, DK, DV), lambda i: (i, 0, 0, 0))
    return pl.pallas_call(
        functools.partial(_mix_sample_kernel, nb=nb, t_len=t_len),
        grid=(bsz // nb,),
        in_specs=[tok(KEY_DIM), tok(KEY_DIM), tok(KEY_DIM), tok(VAL_DIM), st, _whole()],
        out_specs=[tok(VAL_DIM), st],
        out_shape=[jax.ShapeDtypeStruct((bsz, t_len, VAL_DIM), F32),
                   jax.ShapeDtypeStruct((bsz, HEADS, DK, DV), F32)],
        compiler_params=pltpu.CompilerParams(dimension_semantics=("arbitrary",),
                                             vmem_limit_bytes=VMEM_LIMIT),
        name="sample_mix",
    )(q, k, la, v, s0, _tri_const(PACK_ROWS))


def _post_kernel(x_ref, o_ref, cv_ref, p_ref, gpre_ref, wa_ref, wb_ref,
                 ggla_ref, gln_ref, bln_ref, wsq_ref, wpe_ref, gfin_ref, y_ref):
    wao_ref, wbo_ref, wo_ref, wpg_ref = _split_wsq(wsq_ref)
    x = x_ref[...]
    c = D_MODEL
    h = _rms(x, gpre_ref[...]).astype(BF16)
    y_a = _dot(_gated_heads(o_ref[...], _z_a(h, wa_ref), ggla_ref), wao_ref[...])
    szb = _silu(_dot(h, wb_ref[:, 2 * c:3 * c]))
    sga = _sigmoid(_dot(h, wb_ref[:, 3 * c:4 * c]))
    sgb = _sigmoid(_dot(h, wb_ref[:, 4 * c:5 * c]))
    pe = _dot(p_ref[...].astype(BF16), wpe_ref[...])
    y_ref[...] = _tail(x, y_a, cv_ref[...], szb, sga, sgb, pe, gln_ref, bln_ref, wbo_ref, wo_ref, wpg_ref, gfin_ref)


def _post_call(x2d, o, cv, p2d, w, tm):
    n = x2d.shape[0]
    row = lambda wd: pl.BlockSpec((tm, wd), lambda i: (i, 0))
    names = ["gpre", "wa", "wb", "ggla", "gln", "bln", "wsq", "wpe", "gfin"]
    return pl.pallas_call(
        _post_kernel,
        grid=(n // tm,),
        in_specs=[row(D_MODEL)] * 3 + [row(PLE)] + [_whole()] * len(names),
        out_specs=row(D_MODEL),
        out_shape=jax.ShapeDtypeStruct((n, D_MODEL), F32),
        compiler_params=pltpu.CompilerParams(dimension_semantics=("arbitrary",),
                                             vmem_limit_bytes=VMEM_LIMIT),
        name="sample_post",
    )(x2d, o, cv, p2d, *[w[n_] for n_ in names])


def _sample_layer(x, p, s0, c0, w, tm, nb, nbs):
    bsz, t_len, _ = x.shape
    x2d = x.reshape(bsz * t_len, D_MODEL)
    q, k, la, v, u = _pre_call(x2d, w, tm)
    r3 = lambda z: z.reshape(bsz, t_len, z.shape[-1])
    tmaj = lambda z: jnp.transpose(z, (1, 0, 2))
    o, s_new = _mix_sample_call(r3(q), r3(k), r3(la), r3(v), s0, nb)
    cv_t, c_new_t = _sample_conv_call(tmaj(c0), tmaj(r3(u)), w, nbs)
    y = _post_call(x2d, o.reshape(-1, VAL_DIM), tmaj(cv_t).reshape(-1, D_MODEL), p.reshape(bsz * t_len, PLE), w, tm)
    return y.reshape(bsz, t_len, D_MODEL), s_new, tmaj(c_new_t)


def kernel(x_prompt, x_sample, state_gla, state_conv, p_prompt, p_sample, g_pre, w_in, w_a_up, b_a_up, g_gla, w_a_out, w_dw, b_dw, g_ln, b_ln, w_b_out, w_o, w_pe, w_pg, g_final):
    depth = w_in.shape[0]
    assert depth == 1, "one trunk layer"
    i = 0
    tl = _tiles()
    n_a = 2 * KEY_DIM + 2 * VAL_DIM
    row = lambda z: z.reshape(1, -1)
    w = {
        "gpre": row(g_pre[i]),
        "wa": w_in[i][:, :n_a].astype(BF16),
        "wb": w_in[i][:, n_a + RANK:].astype(BF16),
        "wr": jnp.pad(w_in[i][:, n_a:n_a + RANK], ((0, 0), (0, LANES - RANK))).astype(BF16),
        "wup": jnp.pad(w_a_up[i], ((0, LANES - RANK), (0, 0))).astype(BF16),
        "bup": row(b_a_up[i]),
        "tri": _tri_const(CHUNK),
        "wdwb": jnp.repeat(w_dw[i], SUBLANES, axis=0),
        "bdwb": jnp.broadcast_to(row(b_dw[i]), (SUBLANES, D_MODEL)),
        "ggla": row(g_gla[i]), "gln": row(g_ln[i]), "bln": row(b_ln[i]),
        "wsq": jnp.stack([w_a_out[i], w_b_out[i], w_o[i], w_pg[i]]).astype(BF16),
        "wpe": w_pe[i].astype(BF16), "gfin": row(g_final),
    }
    yp, sp, cp = _prompt_call(x_prompt, p_prompt[i], w, tl["tt"])
    ys, ss, cs = _sample_layer(x_sample, p_sample[i], state_gla[i], state_conv[i], w, tl["tm"], tl["nb"],
                               tl["nbs"])
    return (yp, ys, sp[None], cp[None], ss[None], cs[None])
```
